```python
import math
import jax, jax.numpy as jnp
from jax import lax
import numpy as np

D_MODEL = 1024
BATCH = 4
SEQ = 8192
DEPTH = 2

CHUNK = 64
GM_BLOCK = 128
GM_WIDTH = D_MODEL // 2
GM_GROUPS = 4
GM_GROUP_CH = GM_WIDTH // GM_GROUPS
MLA_HEADS = 8
MLA_Q_RANK = D_MODEL // 4
MLA_KV_RANK = D_MODEL // 8
MLA_NOPE = 64
MLA_ROPE = 32
MLA_V = 64
MLA_WIDTH = MLA_HEADS * MLA_V
Q_BLOCK = 128
ROPE_BASE = 10000.0
LRU_WIDTH = D_MODEL // 2
LRU_BLOCKS = 8
LRU_BLOCK_W = LRU_WIDTH // LRU_BLOCKS
CONV_W = 4
LRU_C = 8.0
N_BRANCH = 3
BRANCH_W = GM_WIDTH
D_FF = 4 * D_MODEL
ALPHA = (2.0 * DEPTH) ** 0.25
BETA = (8.0 * DEPTH) ** -0.25
LN_EPS = 1e-5
RMS_EPS = 1e-6
OFF_GM = 0
OFF_QLAT = OFF_GM + 2 * GM_WIDTH
OFF_KVLAT = OFF_QLAT + MLA_Q_RANK
OFF_KROPE = OFF_KVLAT + MLA_KV_RANK
OFF_LRU_X = OFF_KROPE + MLA_ROPE
OFF_LRU_G = OFF_LRU_X + LRU_WIDTH
OFF_GATE = OFF_LRU_G + LRU_WIDTH
N_IN = OFF_GATE + N_BRANCH * D_MODEL

kernel_name = 'hybrid_gmlp_mla_rglru_deepnorm_adaln'


def layer_norm(x):
    xf = x.astype(jnp.float32)
    mu = jnp.mean(xf, axis=-1, keepdims=True)
    var = jnp.mean(jnp.square(xf - mu), axis=-1, keepdims=True)
    return ((xf - mu) * lax.rsqrt(var + LN_EPS)).astype(x.dtype)


def rms_norm(x, g):
    xf = x.astype(jnp.float32)
    y = xf * lax.rsqrt(jnp.mean(jnp.square(xf), axis=-1, keepdims=True) + RMS_EPS)
    return y.astype(x.dtype) * g


def rope_tables(seq, dtype):
    pos = jnp.arange(seq, dtype=jnp.float32)
    inv = ROPE_BASE ** (-jnp.arange(0, MLA_ROPE, 2, dtype=jnp.float32) / MLA_ROPE)
    ang = pos[:, None] * inv[None, :]
    return jnp.cos(ang).astype(dtype), jnp.sin(ang).astype(dtype)


def apply_rope(x, cos, sin):
    x1, x2 = jnp.split(x, 2, axis=-1)
    return jnp.concatenate([x1 * cos - x2 * sin, x2 * cos + x1 * sin], axis=-1)


def gmlp_mix(z, ln_g, ln_b, ws, bs):
    B, S, _ = z.shape
    z = jax.nn.gelu(z)
    u, v = jnp.split(z, 2, axis=-1)
    v = layer_norm(v) * ln_g + ln_b
    v = v.reshape(B, S // GM_BLOCK, GM_BLOCK, GM_GROUPS, GM_GROUP_CH)
    chunk_id = jnp.arange(GM_BLOCK) // CHUNK
    mask = chunk_id[:, None] >= chunk_id[None, :]
    w = jnp.where(mask[None], ws, 0.0)
    f = jnp.einsum('gij,bnjgc->bnigc', w, v) + bs.T[None, None, :, :, None]
    return u * f.reshape(B, S, GM_WIDTH)


def mla_mix(q_lat, kv_lat, k_rope, q_norm_g, w_uq, kv_norm_g, w_ukv, cos, sin):
    B, S, _ = q_lat.shape
    q = (rms_norm(q_lat, q_norm_g) @ w_uq).reshape(B, S, MLA_HEADS, MLA_NOPE + MLA_ROPE)
    q_nope = q[..., :MLA_NOPE]
    q_rope = apply_rope(q[..., MLA_NOPE:], cos[:, None, :], sin[:, None, :])
    kv = (rms_norm(kv_lat, kv_norm_g) @ w_ukv).reshape(B, S, MLA_HEADS, MLA_NOPE + MLA_V)
    k_nope = kv[..., :MLA_NOPE]
    v = kv[..., MLA_NOPE:]
    k_rope = apply_rope(k_rope, cos, sin)
    scale = (MLA_NOPE + MLA_ROPE) ** -0.5
    nq = S // Q_BLOCK
    k_chunk = jnp.arange(S) // CHUNK

    def to_blocks(t):
        return t.reshape(B, nq, Q_BLOCK, *t.shape[2:]).swapaxes(0, 1)

    def attend(args):
        qn, qr, blk = args
        s = (jnp.einsum('bqhd,bkhd->bhqk', qn, k_nope)
             + jnp.einsum('bqhr,bkr->bhqk', qr, k_rope)).astype(jnp.float32) * scale
        q_chunk = (blk * Q_BLOCK + jnp.arange(Q_BLOCK)) // CHUNK
        mask = k_chunk[None, :] <= q_chunk[:, None]
        s = jnp.where(mask[None, None], s, -jnp.inf)
        p = jax.nn.softmax(s, axis=-1).astype(v.dtype)
        return jnp.einsum('bhqk,bkhd->bqhd', p, v)

    o = lax.map(attend, (to_blocks(q_nope), to_blocks(q_rope), jnp.arange(nq)))
    return o.swapaxes(0, 1).reshape(B, S, MLA_WIDTH)


def rglru_mix(xb, gb, conv_w, conv_b, wr, br, wi, bi, lam):
    B, S, _ = xb.shape
    xp = jnp.pad(xb, ((0, 0), (CONV_W - 1, 0), (0, 0)))
    xc = conv_b + sum(xp[:, k:k + S] * conv_w[k] for k in range(CONV_W))
    xblk = xc.reshape(B, S, LRU_BLOCKS, LRU_BLOCK_W)
    r = jax.nn.sigmoid(jnp.einsum('bsnc,ncd->bsnd', xblk, wr).reshape(B, S, LRU_WIDTH) + br)
    i = jax.nn.sigmoid(jnp.einsum('bsnc,ncd->bsnd', xblk, wi).reshape(B, S, LRU_WIDTH) + bi)
    log_a = -LRU_C * r.astype(jnp.float32) * jax.nn.softplus(-lam.astype(jnp.float32))
    a = jnp.exp(log_a)
    b = jnp.sqrt(-jnp.expm1(2.0 * log_a)) * (i * xc).astype(jnp.float32)

    def combine(lhs, rhs):
        a1, b1 = lhs
        a2, b2 = rhs
        return a1 * a2, a2 * b1 + b2

    _, h = lax.associative_scan(combine, (a, b), axis=1)
    return h.astype(xb.dtype) * jax.nn.gelu(gb)


def setup_inputs(seed: int = 0) -> dict:
    key = jax.random.key(seed)
    ks = iter(jax.random.split(key, 40))
    L = DEPTH

    def nrm(shape, scale):
        return jax.random.normal(next(ks), shape, jnp.float32) * scale

    a0 = jax.random.uniform(next(ks), (L, LRU_WIDTH), jnp.float32, 0.9, 0.999)
    return dict(
        x=nrm((BATCH, SEQ, D_MODEL), 1.0),
        c=nrm((BATCH, D_MODEL), 1.0),
        ada_w=nrm((L, D_MODEL, 6 * D_MODEL), 0.1 * D_MODEL ** -0.5),
        ada_b=nrm((L, 6 * D_MODEL), 0.01),
        in_w=nrm((L, D_MODEL, N_IN), D_MODEL ** -0.5),
        in_b=nrm((L, N_IN), 0.01),
        gm_ln_g=1.0 + nrm((L, GM_WIDTH), 0.05),
        gm_ln_b=nrm((L, GM_WIDTH), 0.01),
        gm_ws=nrm((L, GM_GROUPS, GM_BLOCK, GM_BLOCK), GM_BLOCK ** -0.5),
        gm_bs=1.0 + nrm((L, GM_GROUPS, GM_BLOCK), 0.05),
        mla_qnorm_g=1.0 + nrm((L, MLA_Q_RANK), 0.05),
        mla_wuq=nrm((L, MLA_Q_RANK, MLA_HEADS * (MLA_NOPE + MLA_ROPE)), MLA_Q_RANK ** -0.5),
        mla_kvnorm_g=1.0 + nrm((L, MLA_KV_RANK), 0.05),
        mla_wukv=nrm((L, MLA_KV_RANK, MLA_HEADS * (MLA_NOPE + MLA_V)), MLA_KV_RANK ** -0.5),
        lru_conv_w=nrm((L, CONV_W, LRU_WIDTH), CONV_W ** -0.5),
        lru_conv_b=nrm((L, LRU_WIDTH), 0.01),
        lru_wr=nrm((L, LRU_BLOCKS, LRU_BLOCK_W, LRU_BLOCK_W), LRU_BLOCK_W ** -0.5),
        lru_br=nrm((L, LRU_WIDTH), 0.01),
        lru_wi=nrm((L, LRU_BLOCKS, LRU_BLOCK_W, LRU_BLOCK_W), LRU_BLOCK_W ** -0.5),
        lru_bi=nrm((L, LRU_WIDTH), 0.01),
        lru_lambda=jnp.log(a0) - jnp.log1p(-a0),
        branch_w=nrm((L, N_BRANCH, BRANCH_W, D_MODEL), BETA * BRANCH_W ** -0.5),
        mix_out_w=nrm((L, D_MODEL, D_MODEL), BETA * D_MODEL ** -0.5),
        ffn_w1=nrm((L, D_MODEL, D_FF), D_MODEL ** -0.5),
        ffn_b1=nrm((L, D_FF), 0.01),
        ffn_w2=nrm((L, D_FF, D_MODEL), BETA * D_FF ** -0.5),
        ffn_b2=nrm((L, D_MODEL), 0.01),
        ln_g=1.0 + nrm((L, 2, D_MODEL), 0.05),
        ln_b=nrm((L, 2, D_MODEL), 0.01),
    )


def reference(x, c, ada_w, ada_b, in_w, in_b, gm_ln_g, gm_ln_b, gm_ws, gm_bs,
              mla_qnorm_g, mla_wuq, mla_kvnorm_g, mla_wukv,
              lru_conv_w, lru_conv_b, lru_wr, lru_br, lru_wi, lru_bi, lru_lambda,
              branch_w, mix_out_w, ffn_w1, ffn_b1, ffn_w2, ffn_b2, ln_g, ln_b):
    B, S, D = x.shape
    cos, sin = rope_tables(S, x.dtype)
    c_act = jax.nn.silu(c)
    for l in range(DEPTH):
        mod = (c_act @ ada_w[l] + ada_b[l])[:, None, :]
        sh1, sc1, g1, sh2, sc2, g2 = jnp.split(mod, 6, axis=-1)
        h = layer_norm(x) * (1.0 + sc1) + sh1
        z = h @ in_w[l] + in_b[l]
        y_a = gmlp_mix(z[..., OFF_GM:OFF_QLAT], gm_ln_g[l], gm_ln_b[l], gm_ws[l], gm_bs[l])
        y_b = mla_mix(z[..., OFF_QLAT:OFF_KVLAT], z[..., OFF_KVLAT:OFF_KROPE],
                      z[..., OFF_KROPE:OFF_LRU_X], mla_qnorm_g[l], mla_wuq[l],
                      mla_kvnorm_g[l], mla_wukv[l], cos, sin)
        y_c = rglru_mix(z[..., OFF_LRU_X:OFF_LRU_G], z[..., OFF_LRU_G:OFF_GATE],
                        lru_conv_w[l], lru_conv_b[l], lru_wr[l], lru_br[l],
                        lru_wi[l], lru_bi[l], lru_lambda[l])
        gates = jax.nn.sigmoid(z[..., OFF_GATE:]).reshape(B, S, N_BRANCH, D)
        merged = sum(gates[:, :, n] * (y @ branch_w[l, n]) for n, y in enumerate((y_a, y_b, y_c)))
        mix = merged @ mix_out_w[l]
        x = layer_norm(ALPHA * x + (1.0 + g1) * mix) * ln_g[l, 0] + ln_b[l, 0]
        h = layer_norm(x) * (1.0 + sc2) + sh2
        f = jnp.square(jax.nn.relu(h @ ffn_w1[l] + ffn_b1[l])) @ ffn_w2[l] + ffn_b2[l]
        x = layer_norm(ALPHA * x + (1.0 + g2) * f) * ln_g[l, 1] + ln_b[l, 1]
    return x
```

```python
import functools
import math

import jax
import jax.numpy as jnp
import numpy as np
from jax import lax
from jax.experimental import pallas as pl
from jax.experimental.pallas import tpu as pltpu

D_MODEL = 1024
DEPTH = 2
CHUNK = 64
GM_BLOCK = 128
GM_WIDTH = 512
GM_GROUPS = 4
MLA_HEADS = 8
MLA_Q_RANK = 256
MLA_KV_RANK = 128
MLA_NOPE = 64
MLA_ROPE = 32
MLA_V = 64
ROPE_BASE = 10000.0
LRU_WIDTH = 512
LRU_BLOCKS = 8
CONV_W = 4
LRU_C = 8.0
N_BRANCH = 3
D_FF = 4 * D_MODEL
ALPHA = (2.0 * DEPTH) ** 0.25
LN_EPS = 1e-5
RMS_EPS = 1e-6

OFF_QLAT = 2 * GM_WIDTH
OFF_KVLAT = OFF_QLAT + MLA_Q_RANK
OFF_KROPE = OFF_KVLAT + MLA_KV_RANK
OFF_LRU_X = OFF_KROPE + MLA_ROPE
OFF_LRU_G = OFF_LRU_X + LRU_WIDTH
OFF_GATE = OFF_LRU_G + LRU_WIDTH
N_IN = OFF_GATE + N_BRANCH * D_MODEL

HEAD_PAD = 128
QK_W = MLA_HEADS * HEAD_PAD
E_GM = 0
E_QL = E_GM + 2 * GM_WIDTH
E_KVL = E_QL + MLA_Q_RANK
E_KR = E_KVL + MLA_KV_RANK
E_XB = E_KR + 2 * HEAD_PAD
E_GB = E_XB + LRU_WIDTH
E_GATE = E_GB + LRU_WIDTH
E_END = E_GATE + N_BRANCH * D_MODEL

TM = 512
TQ = 1024
TK = 512
TS = 512
VMEM_LIMIT = 56 * 1024 * 1024
NEG_BIG = -1e30
LOG2E = math.log2(math.e)

_bf16 = jnp.bfloat16
_f32 = jnp.float32


def _ln(x):
    mu = jnp.mean(x, axis=-1, keepdims=True)
    xc = x - mu
    var = jnp.mean(xc * xc, axis=-1, keepdims=True)
    return xc * lax.rsqrt(var + LN_EPS)


def _rms(x, g):
    return x * lax.rsqrt(jnp.mean(x * x, axis=-1, keepdims=True) + RMS_EPS) * g


def _dot(a, b):
    return jnp.dot(a, b, preferred_element_type=_f32)


def _dot_nt(a, b):
    return lax.dot_general(a, b, (((1,), (1,)), ((), ())), preferred_element_type=_f32)


def _const_spec(shape):
    nd = len(shape)
    return pl.BlockSpec(shape, lambda *_: (0,) * nd, pipeline_mode=pl.Buffered(1))


def _mod_kernel(c_ref, w_ref, b_ref, o_ref):
    c = c_ref[...]
    c_act = (c * jax.nn.sigmoid(c)).astype(_bf16)
    o_ref[0] = _dot(c_act, w_ref[0].astype(_bf16)) + b_ref[0]


def _modulation(c8, ada_w, ada_b):
    L = ada_w.shape[0]
    nb = 1536
    return pl.pallas_call(
        _mod_kernel,
        out_shape=jax.ShapeDtypeStruct((L, 8, 6 * D_MODEL), _f32),
        grid=(L, 6 * D_MODEL // nb),
        in_specs=[
            pl.BlockSpec((8, D_MODEL), lambda l, j: (0, 0)),
            pl.BlockSpec((1, D_MODEL, nb), lambda l, j: (l, 0, j)),
            pl.BlockSpec((1, 1, nb), lambda l, j: (l, 0, j)),
        ],
        out_specs=pl.BlockSpec((1, 8, nb), lambda l, j: (l, 0, j)),
        compiler_params=pltpu.CompilerParams(
            dimension_semantics=("arbitrary", "arbitrary"), vmem_limit_bytes=VMEM_LIMIT),
        name="adaln_mod",
    )(c8, ada_w, ada_b)


def _in_proj_kernel(x_ref, mod_ref, w_ref, b_ref, gmg_ref, gmb_ref, ws_ref, bst_ref,
                    qg_ref, wuq_ref, kvg_ref, wuk_ref, wuvt_ref, cos_ref, sin_ref,
                    ya_ref, q_ref, k_ref, vt_ref, xb_ref, gg_ref, zg_ref):
    x = x_ref[0]
    h = (_ln(x) * mod_ref[0, 0:1, :] + mod_ref[0, 1:2, :]).astype(_bf16)

    def proj(lo, hi):
        return _dot(h, w_ref[:, lo:hi]) + b_ref[:, lo:hi]

    zg = jax.nn.gelu(proj(E_GM, E_QL))
    u = zg[:, :GM_WIDTH]
    v = (_ln(zg[:, GM_WIDTH:]) * gmg_ref[...] + gmb_ref[...]).astype(_bf16)
    row = lax.broadcasted_iota(jnp.int32, (GM_BLOCK, GM_BLOCK), 0) // CHUNK
    col = lax.broadcasted_iota(jnp.int32, (GM_BLOCK, GM_BLOCK), 1) // CHUNK
    gch = GM_WIDTH // GM_GROUPS
    for g in range(GM_GROUPS):
        wg = jnp.where(row >= col, ws_ref[g], 0.0).astype(_bf16)
        bias = bst_ref[:, g:g + 1]
        for r in range(TM // GM_BLOCK):
            rs = slice(r * GM_BLOCK, (r + 1) * GM_BLOCK)
            cs = slice(g * gch, (g + 1) * gch)
            f = _dot(wg, v[rs, cs]) + bias
            ya_ref[0, rs, cs] = (u[rs, cs] * f).astype(_bf16)

    cos = cos_ref[...]
    sin = sin_ref[...]

    qn = _rms(proj(E_QL, E_KVL), qg_ref[...]).astype(_bf16)
    qscale = (MLA_NOPE + MLA_ROPE) ** -0.5 * LOG2E
    for hd in range(MLA_HEADS):
        hs = slice(hd * HEAD_PAD, (hd + 1) * HEAD_PAD)
        ss = slice(QK_W + hd * HEAD_PAD, QK_W + (hd + 1) * HEAD_PAD)
        qm = _dot(qn, wuq_ref[:, hs])
        qs = _dot(qn, wuq_ref[:, ss])
        q_ref[0, :, hs] = ((qm * cos + qs * sin) * qscale).astype(_bf16)

    ckv = _rms(proj(E_KVL, E_KR), kvg_ref[...]).astype(_bf16)
    kr = proj(E_KR, E_KR + HEAD_PAD) * cos + proj(E_KR + HEAD_PAD, E_XB) * sin
    for hd in range(MLA_HEADS):
        hs = slice(hd * HEAD_PAD, (hd + 1) * HEAD_PAD)
        k_ref[0, :, hs] = (_dot(ckv, wuk_ref[:, hs]) + kr).astype(_bf16)
    vt_ref[0] = _dot_nt(wuvt_ref[...], ckv).astype(_bf16)

    xb_ref[0] = proj(E_XB, E_GB)
    gg_ref[0] = jax.nn.gelu(proj(E_GB, E_GATE)).astype(_bf16)
    for n in range(N_BRANCH):
        lo = E_GATE + n * D_MODEL
        zg_ref[0, :, n * D_MODEL:(n + 1) * D_MODEL] = proj(lo, lo + D_MODEL).astype(_bf16)


def _in_proj(x, mod1, p, cos_t, sin_t):
    B, S, D = x.shape
    tile = lambda w: pl.BlockSpec((1, TM, w), lambda b, i: (b, i, 0))
    outs = [
        jax.ShapeDtypeStruct((B, S, GM_WIDTH), _bf16),
        jax.ShapeDtypeStruct((B, S, QK_W), _bf16),
        jax.ShapeDtypeStruct((B, S, QK_W), _bf16),
        jax.ShapeDtypeStruct((B, MLA_HEADS * MLA_V, S), _bf16),
        jax.ShapeDtypeStruct((B, S, LRU_WIDTH), _f32),
        jax.ShapeDtypeStruct((B, S, LRU_WIDTH), _bf16),
        jax.ShapeDtypeStruct((B, S, N_BRANCH * D_MODEL), _bf16),
    ]
    return pl.pallas_call(
        _in_proj_kernel,
        out_shape=outs,
        grid=(B, S // TM),
        in_specs=[
            tile(D),
            pl.BlockSpec((1, 2, D), lambda b, i: (b, 0, 0)),
            _const_spec((D, E_END)),
            _const_spec((1, E_END)),
            _const_spec((1, GM_WIDTH)),
            _const_spec((1, GM_WIDTH)),
            _const_spec((GM_GROUPS, GM_BLOCK, GM_BLOCK)),
            _const_spec((GM_BLOCK, GM_GROUPS)),
            _const_spec((1, MLA_Q_RANK)),
            _const_spec((MLA_Q_RANK, 2 * QK_W)),
            _const_spec((1, MLA_KV_RANK)),
            _const_spec((MLA_KV_RANK, QK_W)),
            _const_spec((MLA_HEADS * MLA_V, MLA_KV_RANK)),
            pl.BlockSpec((TM, HEAD_PAD), lambda b, i: (i, 0)),
            pl.BlockSpec((TM, HEAD_PAD), lambda b, i: (i, 0)),
        ],
        out_specs=[
            tile(GM_WIDTH), tile(QK_W), tile(QK_W),
            pl.BlockSpec((1, MLA_HEADS * MLA_V, TM), lambda b, i: (b, 0, i)),
            tile(LRU_WIDTH), tile(LRU_WIDTH), tile(N_BRANCH * D_MODEL),
        ],
        compiler_params=pltpu.CompilerParams(
            dimension_semantics=("arbitrary", "arbitrary"), vmem_limit_bytes=VMEM_LIMIT),
        name="in_proj",
    )(x, mod1, p["w_in"], p["b_in"], p["gm_g"], p["gm_b"], p["gm_ws"], p["gm_bst"],
      p["q_g"], p["wuq"], p["kv_g"], p["wuk"], p["wuvt"], cos_t, sin_t)


def _attn_kernel(qi_ref, ki_ref, q_ref, k_ref, vt_ref, o_ref, m_sc, l_sc, acc_sc):
    step = pl.program_id(1)
    qi = qi_ref[step]
    ki = ki_ref[step]

    @pl.when(ki == 0)
    def _():
        m_sc[...] = jnp.full(m_sc.shape, NEG_BIG, _f32)
        l_sc[...] = jnp.zeros(l_sc.shape, _f32)
        acc_sc[...] = jnp.zeros(acc_sc.shape, _f32)

    def update(masked):
        if masked:
            kpos = ki * TK + lax.broadcasted_iota(jnp.int32, (TK, TQ), 0)
            qpos = qi * TQ + lax.broadcasted_iota(jnp.int32, (TK, TQ), 1)
            visible = (kpos // CHUNK) <= (qpos // CHUNK)
        for hd in range(MLA_HEADS):
            hs = slice(hd * HEAD_PAD, (hd + 1) * HEAD_PAD)
            vs = slice(hd * MLA_V, (hd + 1) * MLA_V)
            s = _dot_nt(k_ref[0, :, hs], q_ref[0, :, hs])
            if masked:
                s = jnp.where(visible, s, NEG_BIG)
            m_old = m_sc[hd:hd + 1, :]
            m_new = jnp.maximum(m_old, jnp.max(s, axis=0, keepdims=True))
            alpha = jnp.exp2(m_old - m_new)
            p = jnp.exp2(s - m_new)
            l_sc[hd:hd + 1, :] = alpha * l_sc[hd:hd + 1, :] + jnp.sum(p, axis=0, keepdims=True)
            m_sc[hd:hd + 1, :] = m_new
            pv = _dot(vt_ref[0, vs, :], p.astype(_bf16))
            acc_sc[vs, :] = alpha * acc_sc[vs, :] + pv

    needs_mask = (ki + 1) * TK > qi * TQ

    @pl.when(needs_mask)
    def _():
        update(True)

    @pl.when(jnp.logical_not(needs_mask))
    def _():
        update(False)

    @pl.when((ki + 1) * TK == (qi + 1) * TQ)
    def _():
        for hd in range(MLA_HEADS):
            vs = slice(hd * MLA_V, (hd + 1) * MLA_V)
            acc_sc[vs, :] = acc_sc[vs, :] / l_sc[hd:hd + 1, :]
        o_ref[0] = acc_sc[...].T.astype(_bf16)


def _attention(q, k, vt):
    B, S, _ = q.shape
    nq = S // TQ
    ratio = TQ // TK
    qi_tab, ki_tab = [], []
    for i in range(nq):
        for j in range((i + 1) * ratio):
            qi_tab.append(i)
            ki_tab.append(j)
    n_steps = len(qi_tab)
    qi_tab = jnp.asarray(np.array(qi_tab, np.int32))
    ki_tab = jnp.asarray(np.array(ki_tab, np.int32))
    grid_spec = pltpu.PrefetchScalarGridSpec(
        num_scalar_prefetch=2,
        grid=(B, n_steps),
        in_specs=[
            pl.BlockSpec((1, TQ, QK_W), lambda b, s, qi, ki: (b, qi[s], 0)),
            pl.BlockSpec((1, TK, QK_W), lambda b, s, qi, ki: (b, ki[s], 0)),
            pl.BlockSpec((1, MLA_HEADS * MLA_V, TK), lambda b, s, qi, ki: (b, 0, ki[s])),
        ],
        out_specs=pl.BlockSpec((1, TQ, MLA_HEADS * MLA_V), lambda b, s, qi, ki: (b, qi[s], 0)),
        scratch_shapes=[
            pltpu.VMEM((MLA_HEADS, TQ), _f32),
            pltpu.VMEM((MLA_HEADS, TQ), _f32),
            pltpu.VMEM((MLA_HEADS * MLA_V, TQ), _f32),
        ],
    )
    return pl.pallas_call(
        _attn_kernel,
        out_shape=jax.ShapeDtypeStruct((B, S, MLA_HEADS * MLA_V), _bf16),
        grid_spec=grid_spec,
        compiler_params=pltpu.CompilerParams(
            dimension_semantics=("arbitrary", "arbitrary"), vmem_limit_bytes=VMEM_LIMIT),
        name="mla_attn",
    )(qi_tab, ki_tab, q, k, vt)


def _rglru_kernel(xb_ref, gg_ref, cw_ref, cb_ref, wr_ref, br_ref, wi_ref, bi_ref, lam_ref,
                  y_ref, ext_sc, a_sc, b_sc, carry_sc):
    i = pl.program_id(1)

    @pl.when(i == 0)
    def _():
        ext_sc[0:8, :] = jnp.zeros((8, LRU_WIDTH), _f32)
        carry_sc[...] = jnp.zeros(carry_sc.shape, _f32)

    x = xb_ref[0]
    ext_sc[8:8 + TS, :] = x
    xc = cb_ref[...] + x * cw_ref[CONV_W - 1:CONV_W, :]
    for j in range(1, CONV_W):
        xc = xc + ext_sc[8 - j:8 - j + TS, :] * cw_ref[CONV_W - 1 - j:CONV_W - j, :]
    ext_sc[0:8, :] = x[TS - 8:, :]

    xcb = xc.astype(_bf16)
    r = jax.nn.sigmoid(_dot(xcb, wr_ref[...]) + br_ref[...])
    ig = jax.nn.sigmoid(_dot(xcb, wi_ref[...]) + bi_ref[...])
    lam = lam_ref[...]
    softplus = jnp.maximum(-lam, 0.0) + jnp.log1p(jnp.exp(-jnp.abs(lam)))
    log_a = -LRU_C * r * softplus
    a = jnp.exp(log_a)
    a_sc[...] = a
    b_sc[...] = jnp.sqrt(-jnp.tanh(log_a) * (a * a + 1.0)) * (ig * xc)

    rows = lax.broadcasted_iota(jnp.int32, (8, LRU_WIDTH), 0)

    def body(g, carry):
        off = pl.multiple_of(g * 8, 8)
        a8 = a_sc[pl.ds(off, 8), :]
        b8 = b_sc[pl.ds(off, 8), :]
        for d in (1, 2, 4):
            ok = rows >= d
            a_sh = pltpu.roll(a8, d, 0)
            b_sh = pltpu.roll(b8, d, 0)
            b8 = jnp.where(ok, a8 * b_sh + b8, b8)
            a8 = jnp.where(ok, a8 * a_sh, a8)
        h8 = a8 * carry + b8
        a_sc[pl.ds(off, 8), :] = h8
        return jnp.broadcast_to(h8[7:8, :], (8, LRU_WIDTH))

    carry_sc[...] = lax.fori_loop(0, TS // 8, body, carry_sc[...])
    y_ref[0] = (a_sc[...] * gg_ref[0].astype(_f32)).astype(_bf16)


def _rglru(xb, gg, p):
    B, S, W = xb.shape
    tile = pl.BlockSpec((1, TS, W), lambda b, i: (b, i, 0))
    return pl.pallas_call(
        _rglru_kernel,
        out_shape=jax.ShapeDtypeStruct((B, S, W), _bf16),
        grid=(B, S // TS),
        in_specs=[
            tile, tile,
            _const_spec((CONV_W, W)), _const_spec((1, W)),
            _const_spec((W, W)), _const_spec((1, W)),
            _const_spec((W, W)), _const_spec((1, W)),
            _const_spec((1, W)),
        ],
        out_specs=tile,
        scratch_shapes=[
            pltpu.VMEM((TS + 8, W), _f32),
            pltpu.VMEM((TS, W), _f32),
            pltpu.VMEM((TS, W), _f32),
            pltpu.VMEM((8, W), _f32),
        ],
        compiler_params=pltpu.CompilerParams(
            dimension_semantics=("arbitrary", "arbitrary"), vmem_limit_bytes=VMEM_LIMIT),
        name="rglru",
    )(xb, gg, p["conv_w"], p["conv_b"], p["wr"], p["br"], p["wi"], p["bi"], p["lam"])


def _merge_kernel(x_ref, mod_ref, ya_ref, yb_ref, yc_ref, zg_ref, bw_ref, mw_ref, lng_ref,
                  lnb_ref, o_ref):
    merged = None
    for n, y_ref in enumerate((ya_ref, yb_ref, yc_ref)):
        gate = jax.nn.sigmoid(zg_ref[0, :, n * D_MODEL:(n + 1) * D_MODEL].astype(_f32))
        t = gate * _dot(y_ref[0], bw_ref[n])
        merged = t if merged is None else merged + t
    mix = _dot(merged.astype(_bf16), mw_ref[...])
    r = ALPHA * x_ref[0] + mod_ref[0, 2:3, :] * mix
    o_ref[0] = _ln(r) * lng_ref[...] + lnb_ref[...]


def _merge(x, mod, ya, yb, yc, zg, p):
    B, S, D = x.shape
    tile = lambda w: pl.BlockSpec((1, TM, w), lambda b, i: (b, i, 0))
    return pl.pallas_call(
        _merge_kernel,
        out_shape=jax.ShapeDtypeStruct((B, S, D), _f32),
        grid=(B, S // TM),
        in_specs=[
            tile(D),
            pl.BlockSpec((1, 8, D), lambda b, i: (b, 0, 0)),
            tile(GM_WIDTH), tile(GM_WIDTH), tile(GM_WIDTH), tile(N_BRANCH * D),
            _const_spec((N_BRANCH, GM_WIDTH, D)),
            _const_spec((D, D)),
            _const_spec((1, D)), _const_spec((1, D)),
        ],
        out_specs=tile(D),
        compiler_params=pltpu.CompilerParams(
            dimension_semantics=("arbitrary", "arbitrary"), vmem_limit_bytes=VMEM_LIMIT),
        name="merge",
    )(x, mod, ya, yb, yc, zg, p["branch_w"], p["mix_w"], p["ln_g0"], p["ln_b0"])


FF_CHUNK = 1024


def _ffn_kernel(x_ref, mod_ref, w1_ref, b1_ref, w2_ref, b2_ref, lng_ref, lnb_ref, o_ref):
    x = x_ref[0]
    h = (_ln(x) * mod_ref[0, 4:5, :] + mod_ref[0, 3:4, :]).astype(_bf16)
    f = None
    for c in range(D_FF // FF_CHUNK):
        cs = slice(c * FF_CHUNK, (c + 1) * FF_CHUNK)
        a = jnp.maximum(_dot(h, w1_ref[:, cs]) + b1_ref[:, cs], 0.0)
        t = _dot((a * a).astype(_bf16), w2_ref[cs, :])
        f = t if f is None else f + t
    f = f + b2_ref[...]
    r = ALPHA * x + mod_ref[0, 5:6, :] * f
    o_ref[0] = _ln(r) * lng_ref[...] + lnb_ref[...]


def _ffn(x, mod, p):
    B, S, D = x.shape
    tile = pl.BlockSpec((1, TM, D), lambda b, i: (b, i, 0))
    return pl.pallas_call(
        _ffn_kernel,
        out_shape=jax.ShapeDtypeStruct((B, S, D), _f32),
        grid=(B, S // TM),
        in_specs=[
            tile,
            pl.BlockSpec((1, 8, D), lambda b, i: (b, 0, 0)),
            _const_spec((D, D_FF)), _const_spec((1, D_FF)),
            _const_spec((D_FF, D)), _const_spec((1, D)),
            _const_spec((1, D)), _const_spec((1, D)),
        ],
        out_specs=tile,
        compiler_params=pltpu.CompilerParams(
            dimension_semantics=("arbitrary", "arbitrary"), vmem_limit_bytes=VMEM_LIMIT),
        name="ffn",
    )(x, mod, p["w1"], p["b1"], p["w2"], p["b2"], p["ln_g1"], p["ln_b1"])


def _swap_halves(w):
    half = w.shape[-1] // 2
    return jnp.concatenate([w[..., half:], w[..., :half]], axis=-1)


def _head_pad(nope, rope):
    parts = [nope, rope, jnp.zeros(nope.shape[:-1] + (HEAD_PAD - MLA_NOPE - MLA_ROPE,), nope.dtype)]
    out = jnp.concatenate(parts, axis=-1)
    return out.reshape(out.shape[:-2] + (MLA_HEADS * HEAD_PAD,))


def _block_diag(w):
    n, c, d = w.shape
    eye = jnp.eye(n, dtype=w.dtype)
    return (w[:, :, None, :] * eye[:, None, :, None]).reshape(n * c, n * d)


def _layer_params(l, in_w, in_b, gm_ln_g, gm_ln_b, gm_ws, gm_bs, mla_qnorm_g, mla_wuq,
                  mla_kvnorm_g, mla_wukv, lru_conv_w, lru_conv_b, lru_wr, lru_br, lru_wi,
                  lru_bi, lru_lambda, branch_w, mix_out_w, ffn_w1, ffn_b1, ffn_w2, ffn_b2,
                  ln_g, ln_b):
    def repack(t):
        rows = t.shape[0]
        kr = t[:, OFF_KROPE:OFF_LRU_X]
        z64 = jnp.zeros((rows, MLA_NOPE), t.dtype)
        z32 = jnp.zeros((rows, HEAD_PAD - MLA_NOPE - MLA_ROPE), t.dtype)
        return jnp.concatenate([
            t[:, :OFF_KROPE],
            z64, kr, z32,
            z64, _swap_halves(kr), z32,
            t[:, OFF_LRU_X:],
        ], axis=1)

    wuq = mla_wuq[l].reshape(MLA_Q_RANK, MLA_HEADS, MLA_NOPE + MLA_ROPE)
    zq = jnp.zeros((MLA_Q_RANK, MLA_HEADS, MLA_NOPE), _f32)
    wuq_main = _head_pad(wuq[..., :MLA_NOPE], wuq[..., MLA_NOPE:])
    wuq_swap = _head_pad(zq, _swap_halves(wuq[..., MLA_NOPE:]))
    wukv = mla_wukv[l].reshape(MLA_KV_RANK, MLA_HEADS, MLA_NOPE + MLA_V)
    wuk = _head_pad(wukv[..., :MLA_NOPE], jnp.zeros((MLA_KV_RANK, MLA_HEADS, MLA_ROPE), _f32))
    wuvt = wukv[..., MLA_NOPE:].reshape(MLA_KV_RANK, MLA_HEADS * MLA_V).T
    row = lambda v: v.reshape(1, -1)
    return dict(
        w_in=repack(in_w[l]).astype(_bf16),
        b_in=repack(in_b[l].reshape(1, N_IN)),
        gm_g=row(gm_ln_g[l]), gm_b=row(gm_ln_b[l]),
        gm_ws=gm_ws[l], gm_bst=gm_bs[l].T,
        q_g=row(mla_qnorm_g[l]),
        wuq=jnp.concatenate([wuq_main, wuq_swap], axis=1).astype(_bf16),
        kv_g=row(mla_kvnorm_g[l]),
        wuk=wuk.astype(_bf16), wuvt=wuvt.astype(_bf16),
        conv_w=lru_conv_w[l], conv_b=row(lru_conv_b[l]),
        wr=_block_diag(lru_wr[l]).astype(_bf16), br=row(lru_br[l]),
        wi=_block_diag(lru_wi[l]).astype(_bf16), bi=row(lru_bi[l]),
        lam=row(lru_lambda[l]),
        branch_w=branch_w[l].astype(_bf16), mix_w=mix_out_w[l].astype(_bf16),
        w1=ffn_w1[l].astype(_bf16), b1=row(ffn_b1[l]),
        w2=ffn_w2[l].astype(_bf16), b2=row(ffn_b2[l]),
        ln_g0=row(ln_g[l, 0]), ln_b0=row(ln_b[l, 0]),
        ln_g1=row(ln_g[l, 1]), ln_b1=row(ln_b[l, 1]),
    )


def _rope_tables(seq):
    pos = jnp.arange(seq, dtype=_f32)
    inv = ROPE_BASE ** (-jnp.arange(0, MLA_ROPE, 2, dtype=_f32) / MLA_ROPE)
    ang = pos[:, None] * inv[None, :]
    cos, sin = jnp.cos(ang), jnp.sin(ang)
    ones = jnp.ones((seq, MLA_NOPE), _f32)
    zeros = jnp.zeros((seq, MLA_NOPE), _f32)
    pad = jnp.zeros((seq, HEAD_PAD - MLA_NOPE - MLA_ROPE), _f32)
    cos_t = jnp.concatenate([ones, cos, cos, pad], axis=1)
    sin_t = jnp.concatenate([zeros, -sin, sin, pad], axis=1)
    return cos_t, sin_t


def kernel(x, c, ada_w, ada_b, in_w, in_b, gm_ln_g, gm_ln_b, gm_ws, gm_bs, mla_qnorm_g, mla_wuq, mla_kvnorm_g, mla_wukv, lru_conv_w, lru_conv_b, lru_wr, lru_br, lru_wi, lru_bi, lru_lambda, branch_w, mix_out_w, ffn_w1, ffn_b1, ffn_w2, ffn_b2, ln_g, ln_b):
    B, S, D = x.shape
    assert (B, S, D) == (c.shape[0], S, D_MODEL) and S % TQ == 0 and S % TM == 0
    cos_t, sin_t = _rope_tables(S)
    c8 = jnp.concatenate([c, jnp.zeros((8 - B, D), c.dtype)], axis=0)
    mod_all = _modulation(c8, ada_w, ada_b.reshape(DEPTH, 1, 6 * D))
    for l in range(DEPTH):
        p = _layer_params(l, in_w, in_b, gm_ln_g, gm_ln_b, gm_ws, gm_bs, mla_qnorm_g, mla_wuq,
                          mla_kvnorm_g, mla_wukv, lru_conv_w, lru_conv_b, lru_wr, lru_br,
                          lru_wi, lru_bi, lru_lambda, branch_w, mix_out_w, ffn_w1, ffn_b1,
                          ffn_w2, ffn_b2, ln_g, ln_b)
        m6 = mod_all[l, :B].reshape(B, 6, D)
        one = jnp.array([0.0, 1.0, 1.0, 0.0, 1.0, 1.0], _f32).reshape(1, 6, 1)
        m6 = m6 + one
        mod = jnp.concatenate([m6, jnp.zeros((B, 2, D), _f32)], axis=1)
        mod1 = jnp.stack([mod[:, 1], mod[:, 0]], axis=1)
        ya, q, k, vt, xb, gg, zg = _in_proj(x, mod1, p, cos_t, sin_t)
        yb = _attention(q, k, vt)
        yc = _rglru(xb, gg, p)
        x = _merge(x, mod, ya, yb, yc, zg, p)
        x = _ffn(x, mod, p)
    return x
```

```python
import functools
import math

import jax
import jax.numpy as jnp
import numpy as np
from jax import lax
from jax.experimental import pallas as pl
from jax.experimental.pallas import tpu as pltpu

D_MODEL = 1024
DEPTH = 2
CHUNK = 64
GM_BLOCK = 128
GM_WIDTH = 512
GM_GROUPS = 4
MLA_HEADS = 8
MLA_Q_RANK = 256
MLA_KV_RANK = 128
MLA_NOPE = 64
MLA_ROPE = 32
MLA_V = 64
ROPE_BASE = 10000.0
LRU_WIDTH = 512
LRU_BLOCKS = 8
CONV_W = 4
LRU_C = 8.0
N_BRANCH = 3
D_FF = 4 * D_MODEL
ALPHA = (2.0 * DEPTH) ** 0.25
LN_EPS = 1e-5
RMS_EPS = 1e-6

OFF_QLAT = 2 * GM_WIDTH
OFF_KVLAT = OFF_QLAT + MLA_Q_RANK
OFF_KROPE = OFF_KVLAT + MLA_KV_RANK
OFF_LRU_X = OFF_KROPE + MLA_ROPE
OFF_LRU_G = OFF_LRU_X + LRU_WIDTH
OFF_GATE = OFF_LRU_G + LRU_WIDTH
N_IN = OFF_GATE + N_BRANCH * D_MODEL

HEAD_PAD = 128
QK_W = MLA_HEADS * HEAD_PAD
E_GM = 0
E_QL = E_GM + 2 * GM_WIDTH
E_KVL = E_QL + MLA_Q_RANK
E_KR = E_KVL + MLA_KV_RANK
E_XB = E_KR + 2 * HEAD_PAD
E_GB = E_XB + LRU_WIDTH
E_GATE = E_GB + LRU_WIDTH
E_END = E_GATE + N_BRANCH * D_MODEL

TM = 512
TQ = 1024
TK = 512
TS = 512
VMEM_LIMIT = 56 * 1024 * 1024
NEG_BIG = -1e30
LOG2E = math.log2(math.e)

_bf16 = jnp.bfloat16
_f32 = jnp.float32


def _ln(x):
    mu = jnp.mean(x, axis=-1, keepdims=True)
    xc = x - mu
    var = jnp.mean(xc * xc, axis=-1, keepdims=True)
    return xc * lax.rsqrt(var + LN_EPS)


def _rms(x, g):
    return x * lax.rsqrt(jnp.mean(x * x, axis=-1, keepdims=True) + RMS_EPS) * g


def _sigmoid(x):
    return 0.5 * jnp.tanh(0.5 * x) + 0.5


def _dot(a, b):
    return jnp.dot(a, b, preferred_element_type=_f32)


def _dot_nt(a, b):
    return lax.dot_general(a, b, (((1,), (1,)), ((), ())), preferred_element_type=_f32)


def _const_spec(shape):
    nd = len(shape)
    return pl.BlockSpec(shape, lambda *_: (0,) * nd, pipeline_mode=pl.Buffered(1))


def _mod_kernel(c_ref, w_ref, b_ref, o_ref):
    c = c_ref[...]
    c_act = (c * jax.nn.sigmoid(c)).astype(_bf16)
    o_ref[0] = _dot(c_act, w_ref[0].astype(_bf16)) + b_ref[0]


def _modulation(c8, ada_w, ada_b):
    L = ada_w.shape[0]
    nb = 1536
    return pl.pallas_call(
        _mod_kernel,
        out_shape=jax.ShapeDtypeStruct((L, 8, 6 * D_MODEL), _f32),
        grid=(L, 6 * D_MODEL // nb),
        in_specs=[
            pl.BlockSpec((8, D_MODEL), lambda l, j: (0, 0)),
            pl.BlockSpec((1, D_MODEL, nb), lambda l, j: (l, 0, j)),
            pl.BlockSpec((1, 1, nb), lambda l, j: (l, 0, j)),
        ],
        out_specs=pl.BlockSpec((1, 8, nb), lambda l, j: (l, 0, j)),
        compiler_params=pltpu.CompilerParams(
            dimension_semantics=("arbitrary", "arbitrary"), vmem_limit_bytes=VMEM_LIMIT),
        name="adaln_mod",
    )(c8, ada_w, ada_b)


def _in_proj_kernel(x_ref, mod_ref, w_ref, b_ref, gmg_ref, gmb_ref, ws_ref, bst_ref,
                    qg_ref, wuq_ref, kvg_ref, wuk_ref, wuvt_ref, cos_ref, sin_ref,
                    ya_ref, q_ref, k_ref, vt_ref, xb_ref, gg_ref, zg_ref):
    x = x_ref[0]
    h = (_ln(x) * mod_ref[0, 0:1, :] + mod_ref[0, 1:2, :]).astype(_bf16)

    def proj(lo, hi):
        return _dot(h, w_ref[:, lo:hi]) + b_ref[:, lo:hi]

    zg = jax.nn.gelu(proj(E_GM, E_QL))
    u = zg[:, :GM_WIDTH]
    v = (_ln(zg[:, GM_WIDTH:]) * gmg_ref[...] + gmb_ref[...]).astype(_bf16)
    row = lax.broadcasted_iota(jnp.int32, (GM_BLOCK, GM_BLOCK), 0) // CHUNK
    col = lax.broadcasted_iota(jnp.int32, (GM_BLOCK, GM_BLOCK), 1) // CHUNK
    gch = GM_WIDTH // GM_GROUPS
    for g in range(GM_GROUPS):
        wg = jnp.where(row >= col, ws_ref[g], 0.0).astype(_bf16)
        bias = bst_ref[:, g:g + 1]
        for r in range(TM // GM_BLOCK):
            rs = slice(r * GM_BLOCK, (r + 1) * GM_BLOCK)
            cs = slice(g * gch, (g + 1) * gch)
            f = _dot(wg, v[rs, cs]) + bias
            ya_ref[0, rs, cs] = (u[rs, cs] * f).astype(_bf16)

    cos = cos_ref[...]
    sin = sin_ref[...]

    zmla = proj(E_QL, E_XB)
    qn = _rms(zmla[:, :MLA_Q_RANK], qg_ref[...]).astype(_bf16)
    qscale = (MLA_NOPE + MLA_ROPE) ** -0.5 * LOG2E
    qm = _dot(qn, wuq_ref[:, :QK_W])
    qs = _dot(qn, wuq_ref[:, QK_W:])
    for hd in range(MLA_HEADS):
        hs = slice(hd * HEAD_PAD, (hd + 1) * HEAD_PAD)
        q_ref[0, :, hs] = ((qm[:, hs] * cos + qs[:, hs] * sin) * qscale).astype(_bf16)

    ckv = _rms(zmla[:, E_KVL - E_QL:E_KR - E_QL], kvg_ref[...]).astype(_bf16)
    kr = zmla[:, E_KR - E_QL:E_KR - E_QL + HEAD_PAD] * cos + zmla[:, E_KR - E_QL + HEAD_PAD:] * sin
    kn = _dot(ckv, wuk_ref[...])
    for hd in range(MLA_HEADS):
        hs = slice(hd * HEAD_PAD, (hd + 1) * HEAD_PAD)
        k_ref[0, :, hs] = (kn[:, hs] + kr).astype(_bf16)
    vt_ref[0] = _dot_nt(wuvt_ref[...], ckv).astype(_bf16)

    xb_ref[0] = proj(E_XB, E_GB)
    gg_ref[0] = jax.nn.gelu(proj(E_GB, E_GATE)).astype(_bf16)
    for n in range(N_BRANCH):
        lo = E_GATE + n * D_MODEL
        zg_ref[0, :, n * D_MODEL:(n + 1) * D_MODEL] = proj(lo, lo + D_MODEL).astype(_bf16)


def _in_proj(x, mod1, p, cos_t, sin_t):
    B, S, D = x.shape
    tile = lambda w: pl.BlockSpec((1, TM, w), lambda b, i: (b, i, 0))
    outs = [
        jax.ShapeDtypeStruct((B, S, GM_WIDTH), _bf16),
        jax.ShapeDtypeStruct((B, S, QK_W), _bf16),
        jax.ShapeDtypeStruct((B, S, QK_W), _bf16),
        jax.ShapeDtypeStruct((B, MLA_HEADS * MLA_V, S), _bf16),
        jax.ShapeDtypeStruct((B, S, LRU_WIDTH), _f32),
        jax.ShapeDtypeStruct((B, S, LRU_WIDTH), _bf16),
        jax.ShapeDtypeStruct((B, S, N_BRANCH * D_MODEL), _bf16),
    ]
    return pl.pallas_call(
        _in_proj_kernel,
        out_shape=outs,
        grid=(B, S // TM),
        in_specs=[
            tile(D),
            pl.BlockSpec((1, 2, D), lambda b, i: (b, 0, 0)),
            _const_spec((D, E_END)),
            _const_spec((1, E_END)),
            _const_spec((1, GM_WIDTH)),
            _const_spec((1, GM_WIDTH)),
            _const_spec((GM_GROUPS, GM_BLOCK, GM_BLOCK)),
            _const_spec((GM_BLOCK, GM_GROUPS)),
            _const_spec((1, MLA_Q_RANK)),
            _const_spec((MLA_Q_RANK, 2 * QK_W)),
            _const_spec((1, MLA_KV_RANK)),
            _const_spec((MLA_KV_RANK, QK_W)),
            _const_spec((MLA_HEADS * MLA_V, MLA_KV_RANK)),
            pl.BlockSpec((TM, HEAD_PAD), lambda b, i: (i, 0)),
            pl.BlockSpec((TM, HEAD_PAD), lambda b, i: (i, 0)),
        ],
        out_specs=[
            tile(GM_WIDTH), tile(QK_W), tile(QK_W),
            pl.BlockSpec((1, MLA_HEADS * MLA_V, TM), lambda b, i: (b, 0, i)),
            tile(LRU_WIDTH), tile(LRU_WIDTH), tile(N_BRANCH * D_MODEL),
        ],
        compiler_params=pltpu.CompilerParams(
            dimension_semantics=("arbitrary", "arbitrary"), vmem_limit_bytes=VMEM_LIMIT),
        name="in_proj",
    )(x, mod1, p["w_in"], p["b_in"], p["gm_g"], p["gm_b"], p["gm_ws"], p["gm_bst"],
      p["q_g"], p["wuq"], p["kv_g"], p["wuk"], p["wuvt"], cos_t, sin_t)


SUB = 256
ACC_ROWS = MLA_V + 16
LAG_STATS, LAG_PV, LAG_ACC = 3, 5, 8


def _attn_units(rel):
    units = []
    for kh in range(TK // SUB):
        for hd in range(MLA_HEADS):
            for c in range(TQ // SUB):
                mode = "full"
                if rel is not None:
                    koff, qoff = rel + kh * SUB, c * SUB
                    if koff >= qoff + SUB:
                        continue
                    if koff + SUB > qoff:
                        assert koff == qoff
                        mode = "diag"
                units.append((kh, hd, c, mode))
    return units


def _attn_kernel(qi_ref, ki_ref, q_ref, k_ref, vt_ref, o_ref, m_sc, acc_sc, o_sc):
    step = pl.program_id(1)
    qi = qi_ref[step]
    ki = ki_ref[step]
    nsub = SUB // 8

    @pl.when(ki == 0)
    def _():
        m_sc[...] = jnp.full(m_sc.shape, NEG_BIG, _f32)
        acc_sc[...] = jnp.zeros(acc_sc.shape, _f32)

    def stage_qk(unit, st, visible):
        kh, hd, c, mode = unit
        hs = slice(hd * HEAD_PAD, (hd + 1) * HEAD_PAD)
        s = _dot_nt(k_ref[0, kh * SUB:(kh + 1) * SUB, hs], q_ref[0, c * SUB:(c + 1) * SUB, hs])
        if mode == "diag":
            s = jnp.where(visible, s, NEG_BIG)
        st["s3"] = s.reshape(nsub, 8, SUB)

    def stage_stats(unit, st, visible):
        kh, hd, c, mode = unit
        cm = jnp.max(st["s3"], axis=0)
        for d in (4, 2, 1):
            cm = jnp.maximum(cm, pltpu.roll(cm, d, 0))
        msl = (slice(hd * 8, (hd + 1) * 8), slice(c * SUB, (c + 1) * SUB))
        m_old = m_sc[msl]
        m_new = jnp.maximum(m_old, cm)
        m_sc[msl] = m_new
        st["alpha"] = jnp.exp2(m_old - m_new)
        st["p"] = jnp.exp2(st.pop("s3") - m_new[None]).reshape(SUB, SUB).astype(_bf16)

    def stage_pv(unit, st, visible):
        kh, hd, c, _ = unit
        v_t = vt_ref[0, hd * MLA_V:(hd + 1) * MLA_V, kh * SUB:(kh + 1) * SUB]
        v_ext = jnp.concatenate([v_t, jnp.ones((ACC_ROWS - MLA_V, SUB), _bf16)], axis=0)
        st["pv"] = _dot(v_ext, st.pop("p")).reshape(ACC_ROWS // 8, 8, SUB)

    def stage_acc(unit, st, visible):
        kh, hd, c, _ = unit
        cs = slice(c * SUB, (c + 1) * SUB)
        acc = acc_sc[hd, :, cs].reshape(ACC_ROWS // 8, 8, SUB)
        acc_sc[hd, :, cs] = (st.pop("alpha")[None] * acc + st.pop("pv")).reshape(ACC_ROWS, SUB)

    stages = ((stage_qk, 0), (stage_stats, LAG_STATS), (stage_pv, LAG_PV), (stage_acc, LAG_ACC))

    def update(rel):
        units = _attn_units(rel)
        visible = None
        if rel is not None:
            row = lax.broadcasted_iota(jnp.int32, (SUB, SUB), 0) // CHUNK
            col = lax.broadcasted_iota(jnp.int32, (SUB, SUB), 1) // CHUNK
            visible = row <= col
        n = len(units)
        state = [{} for _ in units]
        for t in range(n + LAG_ACC):
            for fn, lag in stages:
                if 0 <= t - lag < n:
                    fn(units[t - lag], state[t - lag], visible)

    rel = ki * TK - qi * TQ
    for r in range(0, TQ, TK):
        pl.when(rel == r)(functools.partial(update, r))
    pl.when(rel < 0)(functools.partial(update, None))

    @pl.when(rel == TQ - TK)
    def _():
        for hd in range(MLA_HEADS):
            inv = 1.0 / acc_sc[hd, MLA_V:MLA_V + 8, :]
            o = acc_sc[hd, 0:MLA_V, :].reshape(MLA_V // 8, 8, TQ) * inv[None]
            o_sc[hd * MLA_V:(hd + 1) * MLA_V, :] = o.reshape(MLA_V, TQ)
        o_ref[0] = o_sc[...].T.astype(_bf16)


def _attention(q, k, vt):
    B, S, _ = q.shape
    nq = S // TQ
    ratio = TQ // TK
    qi_tab, ki_tab = [], []
    for i in range(nq):
        for j in range((i + 1) * ratio):
            qi_tab.append(i)
            ki_tab.append(j)
    n_steps = len(qi_tab)
    qi_tab = jnp.asarray(np.array(qi_tab, np.int32))
    ki_tab = jnp.asarray(np.array(ki_tab, np.int32))
    grid_spec = pltpu.PrefetchScalarGridSpec(
        num_scalar_prefetch=2,
        grid=(B, n_steps),
        in_specs=[
            pl.BlockSpec((1, TQ, QK_W), lambda b, s, qi, ki: (b, qi[s], 0)),
            pl.BlockSpec((1, TK, QK_W), lambda b, s, qi, ki: (b, ki[s], 0)),
            pl.BlockSpec((1, MLA_HEADS * MLA_V, TK), lambda b, s, qi, ki: (b, 0, ki[s])),
        ],
        out_specs=pl.BlockSpec((1, TQ, MLA_HEADS * MLA_V), lambda b, s, qi, ki: (b, qi[s], 0)),
        scratch_shapes=[
            pltpu.VMEM((MLA_HEADS * 8, TQ), _f32),
            pltpu.VMEM((MLA_HEADS, ACC_ROWS, TQ), _f32),
            pltpu.VMEM((MLA_HEADS * MLA_V, TQ), _f32),
        ],
    )
    return pl.pallas_call(
        _attn_kernel,
        out_shape=jax.ShapeDtypeStruct((B, S, MLA_HEADS * MLA_V), _bf16),
        grid_spec=grid_spec,
        compiler_params=pltpu.CompilerParams(
            dimension_semantics=("arbitrary", "arbitrary"), vmem_limit_bytes=VMEM_LIMIT),
        name="mla_attn",
    )(qi_tab, ki_tab, q, k, vt)


def _rglru_kernel(xb_ref, gg_ref, cw_ref, cb_ref, wr_ref, br_ref, wi_ref, bi_ref, lam_ref,
                  y_ref, ext_sc, a_sc, b_sc, carry_sc):
    i = pl.program_id(1)

    @pl.when(i == 0)
    def _():
        ext_sc[0:8, :] = jnp.zeros((8, LRU_WIDTH), _f32)
        carry_sc[...] = jnp.zeros(carry_sc.shape, _f32)

    x = xb_ref[0]
    ext_sc[8:8 + TS, :] = x
    xc = cb_ref[...] + x * cw_ref[CONV_W - 1:CONV_W, :]
    for j in range(1, CONV_W):
        xc = xc + ext_sc[8 - j:8 - j + TS, :] * cw_ref[CONV_W - 1 - j:CONV_W - j, :]
    ext_sc[0:8, :] = x[TS - 8:, :]

    xcb = xc.astype(_bf16)
    r = _sigmoid(_dot(xcb, wr_ref[...]) + br_ref[...])
    ig = _sigmoid(_dot(xcb, wi_ref[...]) + bi_ref[...])
    lam = lam_ref[...]
    softplus = jnp.maximum(-lam, 0.0) + jnp.log1p(jnp.exp(-jnp.abs(lam)))
    log_a = -LRU_C * r * softplus
    a = jnp.exp(log_a)
    a_sc[...] = a
    b_sc[...] = jnp.sqrt(-jnp.tanh(log_a) * (a * a + 1.0)) * (ig * xc)

    rows = lax.broadcasted_iota(jnp.int32, (8, LRU_WIDTH), 0)

    def body(g, carry):
        off = pl.multiple_of(g * 8, 8)
        a8 = a_sc[pl.ds(off, 8), :]
        b8 = b_sc[pl.ds(off, 8), :]
        for d in (1, 2, 4):
            ok = rows >= d
            a_sh = pltpu.roll(a8, d, 0)
            b_sh = pltpu.roll(b8, d, 0)
            b8 = jnp.where(ok, a8 * b_sh + b8, b8)
            a8 = jnp.where(ok, a8 * a_sh, a8)
        h8 = a8 * carry + b8
        a_sc[pl.ds(off, 8), :] = h8
        return jnp.broadcast_to(h8[7:8, :], (8, LRU_WIDTH))

    carry_sc[...] = lax.fori_loop(0, TS // 8, body, carry_sc[...])
    y_ref[0] = (a_sc[...] * gg_ref[0].astype(_f32)).astype(_bf16)


def _rglru(xb, gg, p):
    B, S, W = xb.shape
    tile = pl.BlockSpec((1, TS, W), lambda b, i: (b, i, 0))
    return pl.pallas_call(
        _rglru_kernel,
        out_shape=jax.ShapeDtypeStruct((B, S, W), _bf16),
        grid=(B, S // TS),
        in_specs=[
            tile, tile,
            _const_spec((CONV_W, W)), _const_spec((1, W)),
            _const_spec((W, W)), _const_spec((1, W)),
            _const_spec((W, W)), _const_spec((1, W)),
            _const_spec((1, W)),
        ],
        out_specs=tile,
        scratch_shapes=[
            pltpu.VMEM((TS + 8, W), _f32),
            pltpu.VMEM((TS, W), _f32),
            pltpu.VMEM((TS, W), _f32),
            pltpu.VMEM((8, W), _f32),
        ],
        compiler_params=pltpu.CompilerParams(
            dimension_semantics=("arbitrary", "arbitrary"), vmem_limit_bytes=VMEM_LIMIT),
        name="rglru",
    )(xb, gg, p["conv_w"], p["conv_b"], p["wr"], p["br"], p["wi"], p["bi"], p["lam"])


def _merge_kernel(x_ref, mod_ref, ya_ref, yb_ref, yc_ref, zg_ref, bw_ref, mw_ref, lng_ref,
                  lnb_ref, o_ref):
    merged = None
    for n, y_ref in enumerate((ya_ref, yb_ref, yc_ref)):
        gate = _sigmoid(zg_ref[0, :, n * D_MODEL:(n + 1) * D_MODEL].astype(_f32))
        t = gate * _dot(y_ref[0], bw_ref[n])
        merged = t if merged is None else merged + t
    mix = _dot(merged.astype(_bf16), mw_ref[...])
    r = ALPHA * x_ref[0] + mod_ref[0, 2:3, :] * mix
    o_ref[0] = _ln(r) * lng_ref[...] + lnb_ref[...]


def _merge(x, mod, ya, yb, yc, zg, p):
    B, S, D = x.shape
    tile = lambda w: pl.BlockSpec((1, TM, w), lambda b, i: (b, i, 0))
    return pl.pallas_call(
        _merge_kernel,
        out_shape=jax.ShapeDtypeStruct((B, S, D), _f32),
        grid=(B, S // TM),
        in_specs=[
            tile(D),
            pl.BlockSpec((1, 8, D), lambda b, i: (b, 0, 0)),
            tile(GM_WIDTH), tile(GM_WIDTH), tile(GM_WIDTH), tile(N_BRANCH * D),
            _const_spec((N_BRANCH, GM_WIDTH, D)),
            _const_spec((D, D)),
            _const_spec((1, D)), _const_spec((1, D)),
        ],
        out_specs=tile(D),
        compiler_params=pltpu.CompilerParams(
            dimension_semantics=("arbitrary", "arbitrary"), vmem_limit_bytes=VMEM_LIMIT),
        name="merge",
    )(x, mod, ya, yb, yc, zg, p["branch_w"], p["mix_w"], p["ln_g0"], p["ln_b0"])


FF_CHUNK = 1024


def _ffn_kernel(x_ref, mod_ref, w1_ref, b1_ref, w2_ref, b2_ref, lng_ref, lnb_ref, o_ref):
    x = x_ref[0]
    h = (_ln(x) * mod_ref[0, 4:5, :] + mod_ref[0, 3:4, :]).astype(_bf16)
    f = None
    for c in range(D_FF // FF_CHUNK):
        cs = slice(c * FF_CHUNK, (c + 1) * FF_CHUNK)
        a = jnp.maximum(_dot(h, w1_ref[:, cs]) + b1_ref[:, cs], 0.0)
        t = _dot((a * a).astype(_bf16), w2_ref[cs, :])
        f = t if f is None else f + t
    f = f + b2_ref[...]
    r = ALPHA * x + mod_ref[0, 5:6, :] * f
    o_ref[0] = _ln(r) * lng_ref[...] + lnb_ref[...]


def _ffn(x, mod, p):
    B, S, D = x.shape
    tile = pl.BlockSpec((1, TM, D), lambda b, i: (b, i, 0))
    return pl.pallas_call(
        _ffn_kernel,
        out_shape=jax.ShapeDtypeStruct((B, S, D), _f32),
        grid=(B, S // TM),
        in_specs=[
            tile,
            pl.BlockSpec((1, 8, D), lambda b, i: (b, 0, 0)),
            _const_spec((D, D_FF)), _const_spec((1, D_FF)),
            _const_spec((D_FF, D)), _const_spec((1, D)),
            _const_spec((1, D)), _const_spec((1, D)),
        ],
        out_specs=tile,
        compiler_params=pltpu.CompilerParams(
            dimension_semantics=("arbitrary", "arbitrary"), vmem_limit_bytes=VMEM_LIMIT),
        name="ffn",
    )(x, mod, p["w1"], p["b1"], p["w2"], p["b2"], p["ln_g1"], p["ln_b1"])


def _swap_halves(w):
    half = w.shape[-1] // 2
    return jnp.concatenate([w[..., half:], w[..., :half]], axis=-1)


def _head_pad(nope, rope):
    parts = [nope, rope, jnp.zeros(nope.shape[:-1] + (HEAD_PAD - MLA_NOPE - MLA_ROPE,), nope.dtype)]
    out = jnp.concatenate(parts, axis=-1)
    return out.reshape(out.shape[:-2] + (MLA_HEADS * HEAD_PAD,))


def _block_diag(w):
    n, c, d = w.shape
    eye = jnp.eye(n, dtype=w.dtype)
    return (w[:, :, None, :] * eye[:, None, :, None]).reshape(n * c, n * d)


def _layer_params(l, in_w, in_b, gm_ln_g, gm_ln_b, gm_ws, gm_bs, mla_qnorm_g, mla_wuq,
                  mla_kvnorm_g, mla_wukv, lru_conv_w, lru_conv_b, lru_wr, lru_br, lru_wi,
                  lru_bi, lru_lambda, branch_w, mix_out_w, ffn_w1, ffn_b1, ffn_w2, ffn_b2,
                  ln_g, ln_b):
    def repack(t):
        rows = t.shape[0]
        kr = t[:, OFF_KROPE:OFF_LRU_X]
        z64 = jnp.zeros((rows, MLA_NOPE), t.dtype)
        z32 = jnp.zeros((rows, HEAD_PAD - MLA_NOPE - MLA_ROPE), t.dtype)
        return jnp.concatenate([
            t[:, :OFF_KROPE],
            z64, kr, z32,
            z64, _swap_halves(kr), z32,
            t[:, OFF_LRU_X:],
        ], axis=1)

    wuq = mla_wuq[l].reshape(MLA_Q_RANK, MLA_HEADS, MLA_NOPE + MLA_ROPE)
    zq = jnp.zeros((MLA_Q_RANK, MLA_HEADS, MLA_NOPE), _f32)
    wuq_main = _head_pad(wuq[..., :MLA_NOPE], wuq[..., MLA_NOPE:])
    wuq_swap = _head_pad(zq, _swap_halves(wuq[..., MLA_NOPE:]))
    wukv = mla_wukv[l].reshape(MLA_KV_RANK, MLA_HEADS, MLA_NOPE + MLA_V)
    wuk = _head_pad(wukv[..., :MLA_NOPE], jnp.zeros((MLA_KV_RANK, MLA_HEADS, MLA_ROPE), _f32))
    wuvt = wukv[..., MLA_NOPE:].reshape(MLA_KV_RANK, MLA_HEADS * MLA_V).T
    row = lambda v: v.reshape(1, -1)
    return dict(
        w_in=repack(in_w[l]).astype(_bf16),
        b_in=repack(in_b[l].reshape(1, N_IN)),
        gm_g=row(gm_ln_g[l]), gm_b=row(gm_ln_b[l]),
        gm_ws=gm_ws[l], gm_bst=gm_bs[l].T,
        q_g=row(mla_qnorm_g[l]),
        wuq=jnp.concatenate([wuq_main, wuq_swap], axis=1).astype(_bf16),
        kv_g=row(mla_kvnorm_g[l]),
        wuk=wuk.astype(_bf16), wuvt=wuvt.astype(_bf16),
        conv_w=lru_conv_w[l], conv_b=row(lru_conv_b[l]),
        wr=_block_diag(lru_wr[l]).astype(_bf16), br=row(lru_br[l]),
        wi=_block_diag(lru_wi[l]).astype(_bf16), bi=row(lru_bi[l]),
        lam=row(lru_lambda[l]),
        branch_w=branch_w[l].astype(_bf16), mix_w=mix_out_w[l].astype(_bf16),
        w1=ffn_w1[l].astype(_bf16), b1=row(ffn_b1[l]),
        w2=ffn_w2[l].astype(_bf16), b2=row(ffn_b2[l]),
        ln_g0=row(ln_g[l, 0]), ln_b0=row(ln_b[l, 0]),
        ln_g1=row(ln_g[l, 1]), ln_b1=row(ln_b[l, 1]),
    )


def _rope_tables(seq):
    pos = jnp.arange(seq, dtype=_f32)
    inv = ROPE_BASE ** (-jnp.arange(0, MLA_ROPE, 2, dtype=_f32) / MLA_ROPE)
    ang = pos[:, None] * inv[None, :]
    cos, sin = jnp.cos(ang), jnp.sin(ang)
    ones = jnp.ones((seq, MLA_NOPE), _f32)
    zeros = jnp.zeros((seq, MLA_NOPE), _f32)
    pad = jnp.zeros((seq, HEAD_PAD - MLA_NOPE - MLA_ROPE), _f32)
    cos_t = jnp.concatenate([ones, cos, cos, pad], axis=1)
    sin_t = jnp.concatenate([zeros, -sin, sin, pad], axis=1)
    return cos_t, sin_t


def kernel(x, c, ada_w, ada_b, in_w, in_b, gm_ln_g, gm_ln_b, gm_ws, gm_bs, mla_qnorm_g, mla_wuq, mla_kvnorm_g, mla_wukv, lru_conv_w, lru_conv_b, lru_wr, lru_br, lru_wi, lru_bi, lru_lambda, branch_w, mix_out_w, ffn_w1, ffn_b1, ffn_w2, ffn_b2, ln_g, ln_b):
    B, S, D = x.shape
    assert (B, S, D) == (c.shape[0], S, D_MODEL) and S % TQ == 0 and S % TM == 0
    cos_t, sin_t = _rope_tables(S)
    c8 = jnp.concatenate([c, jnp.zeros((8 - B, D), c.dtype)], axis=0)
    mod_all = _modulation(c8, ada_w, ada_b.reshape(DEPTH, 1, 6 * D))
    for l in range(DEPTH):
        p = _layer_params(l, in_w, in_b, gm_ln_g, gm_ln_b, gm_ws, gm_bs, mla_qnorm_g, mla_wuq,
                          mla_kvnorm_g, mla_wukv, lru_conv_w, lru_conv_b, lru_wr, lru_br,
                          lru_wi, lru_bi, lru_lambda, branch_w, mix_out_w, ffn_w1, ffn_b1,
                          ffn_w2, ffn_b2, ln_g, ln_b)
        m6 = mod_all[l, :B].reshape(B, 6, D)
        one = jnp.array([0.0, 1.0, 1.0, 0.0, 1.0, 1.0], _f32).reshape(1, 6, 1)
        m6 = m6 + one
        mod = jnp.concatenate([m6, jnp.zeros((B, 2, D), _f32)], axis=1)
        mod1 = jnp.stack([mod[:, 1], mod[:, 0]], axis=1)
        ya, q, k, vt, xb, gg, zg = _in_proj(x, mod1, p, cos_t, sin_t)
        yb = _attention(q, k, vt)
        yc = _rglru(xb, gg, p)
        x = _merge(x, mod, ya, yb, yc, zg, p)
        x = _ffn(x, mod, p)
    return x
```

```python
import functools
import math

import jax
import jax.numpy as jnp
import numpy as np
from jax import lax
from jax.experimental import pallas as pl
from jax.experimental.pallas import tpu as pltpu

D_MODEL = 1024
DEPTH = 2
CHUNK = 64
GM_BLOCK = 128
GM_WIDTH = 512
GM_GROUPS = 4
MLA_HEADS = 8
MLA_Q_RANK = 256
MLA_KV_RANK = 128
MLA_NOPE = 64
MLA_ROPE = 32
MLA_V = 64
ROPE_BASE = 10000.0
LRU_WIDTH = 512
LRU_BLOCKS = 8
CONV_W = 4
LRU_C = 8.0
N_BRANCH = 3
D_FF = 4 * D_MODEL
ALPHA = (2.0 * DEPTH) ** 0.25
LN_EPS = 1e-5
RMS_EPS = 1e-6

OFF_QLAT = 2 * GM_WIDTH
OFF_KVLAT = OFF_QLAT + MLA_Q_RANK
OFF_KROPE = OFF_KVLAT + MLA_KV_RANK
OFF_LRU_X = OFF_KROPE + MLA_ROPE
OFF_LRU_G = OFF_LRU_X + LRU_WIDTH
OFF_GATE = OFF_LRU_G + LRU_WIDTH
N_IN = OFF_GATE + N_BRANCH * D_MODEL

HEAD_PAD = 128
QK_W = MLA_HEADS * HEAD_PAD
OFF_GM = 0

TM = 512
TQ = 1024
TK = 512
TS = 512
VMEM_LIMIT = 56 * 1024 * 1024
NEG_BIG = -1e30
LOG2E = math.log2(math.e)

_bf16 = jnp.bfloat16
_f32 = jnp.float32


def _ln(x):
    mu = jnp.mean(x, axis=-1, keepdims=True)
    xc = x - mu
    var = jnp.mean(xc * xc, axis=-1, keepdims=True)
    return xc * lax.rsqrt(var + LN_EPS)


def _rms(x, g):
    return x * lax.rsqrt(jnp.mean(x * x, axis=-1, keepdims=True) + RMS_EPS) * g


def _sigmoid(x):
    return 0.5 * jnp.tanh(0.5 * x) + 0.5


def _dot(a, b):
    return jnp.dot(a, b, preferred_element_type=_f32)


def _dot_nt(a, b):
    return lax.dot_general(a, b, (((1,), (1,)), ((), ())), preferred_element_type=_f32)


def _mod_spec(l):
    return pl.BlockSpec((None, 1, 8, D_MODEL), lambda b, i: (l, b, 0, 0))


def _layer_spec(l, shape):
    nd = len(shape)
    return pl.BlockSpec((None,) + tuple(shape), lambda *_: (l,) + (0,) * nd,
                        pipeline_mode=pl.Buffered(1))


def _mod_kernel(c_ref, w_ref, b_ref, o_ref):
    c = c_ref[...]
    c_act = (c * jax.nn.sigmoid(c)).astype(_bf16)
    o_ref[0] = _dot(c_act, w_ref[0].astype(_bf16)) + b_ref[0]


def _modulation(c8, ada_w, ada_b):
    L = ada_w.shape[0]
    nb = 1536
    return pl.pallas_call(
        _mod_kernel,
        out_shape=jax.ShapeDtypeStruct((L, 8, 6 * D_MODEL), _f32),
        grid=(L, 6 * D_MODEL // nb),
        in_specs=[
            pl.BlockSpec((8, D_MODEL), lambda l, j: (0, 0)),
            pl.BlockSpec((1, D_MODEL, nb), lambda l, j: (l, 0, j)),
            pl.BlockSpec((1, 1, nb), lambda l, j: (l, 0, j)),
        ],
        out_specs=pl.BlockSpec((1, 8, nb), lambda l, j: (l, 0, j)),
        compiler_params=pltpu.CompilerParams(
            dimension_semantics=("arbitrary", "arbitrary"), vmem_limit_bytes=VMEM_LIMIT),
        name="adaln_mod",
    )(c8, ada_w, ada_b)


def _in_proj_kernel(x_ref, mod_ref, wa_ref, ba_ref, wkr_ref, bkr_ref, wb_ref, bb_ref,
                    gmg_ref, gmb_ref, ws_ref, bst_ref,
                    qg_ref, wuq_ref, kvg_ref, wuk_ref, wuvt_ref, cos_ref, sin_ref,
                    ya_ref, q_ref, k_ref, vt_ref, xb_ref, gg_ref, zg_ref):
    x = x_ref[0]
    h = (_ln(x) * mod_ref[0, 1:2, :] + mod_ref[0, 0:1, :]).astype(_bf16)

    def proj(w_ref, b_ref, lo, hi):
        return _dot(h, w_ref[:, lo:hi]) + b_ref[:, lo:hi]

    zg = jax.nn.gelu(proj(wa_ref, ba_ref, OFF_GM, OFF_QLAT))
    u = zg[:, :GM_WIDTH]
    v = (_ln(zg[:, GM_WIDTH:]) * gmg_ref[...] + gmb_ref[...]).astype(_bf16)
    row = lax.broadcasted_iota(jnp.int32, (GM_BLOCK, GM_BLOCK), 0) // CHUNK
    col = lax.broadcasted_iota(jnp.int32, (GM_BLOCK, GM_BLOCK), 1) // CHUNK
    gch = GM_WIDTH // GM_GROUPS
    for g in range(GM_GROUPS):
        wg = jnp.where(row >= col, ws_ref[g], 0.0).astype(_bf16)
        bias = bst_ref[:, g:g + 1]
        for r in range(TM // GM_BLOCK):
            rs = slice(r * GM_BLOCK, (r + 1) * GM_BLOCK)
            cs = slice(g * gch, (g + 1) * gch)
            f = _dot(wg, v[rs, cs]) + bias
            ya_ref[0, rs, cs] = (u[rs, cs] * f).astype(_bf16)

    cos = cos_ref[...]
    sin = sin_ref[...]

    zlat = proj(wa_ref, ba_ref, OFF_QLAT, OFF_KROPE)
    zkr = proj(wkr_ref, bkr_ref, 0, 2 * HEAD_PAD)
    qn = _rms(zlat[:, :MLA_Q_RANK], qg_ref[...]).astype(_bf16)
    qscale = (MLA_NOPE + MLA_ROPE) ** -0.5 * LOG2E
    qm = _dot(qn, wuq_ref[:, :QK_W])
    qs = _dot(qn, wuq_ref[:, QK_W:])
    for hd in range(MLA_HEADS):
        hs = slice(hd * HEAD_PAD, (hd + 1) * HEAD_PAD)
        q_ref[0, :, hs] = ((qm[:, hs] * cos + qs[:, hs] * sin) * qscale).astype(_bf16)

    ckv = _rms(zlat[:, MLA_Q_RANK:], kvg_ref[...]).astype(_bf16)
    kr = zkr[:, :HEAD_PAD] * cos + zkr[:, HEAD_PAD:] * sin
    kn = _dot(ckv, wuk_ref[...])
    for hd in range(MLA_HEADS):
        hs = slice(hd * HEAD_PAD, (hd + 1) * HEAD_PAD)
        k_ref[0, :, hs] = (kn[:, hs] + kr).astype(_bf16)
    vt_ref[0] = _dot_nt(wuvt_ref[...], ckv).astype(_bf16)

    xb_ref[0] = proj(wb_ref, bb_ref, 0, LRU_WIDTH)
    gg_ref[0] = jax.nn.gelu(proj(wb_ref, bb_ref, LRU_WIDTH, 2 * LRU_WIDTH)).astype(_bf16)
    for n in range(N_BRANCH):
        lo = 2 * LRU_WIDTH + n * D_MODEL
        zg_ref[0, :, n * D_MODEL:(n + 1) * D_MODEL] = proj(wb_ref, bb_ref, lo, lo + D_MODEL).astype(_bf16)


def _in_proj(l, x, mod, p, cos_t, sin_t):
    B, S, D = x.shape
    tile = lambda w: pl.BlockSpec((1, TM, w), lambda b, i: (b, i, 0))
    outs = [
        jax.ShapeDtypeStruct((B, S, GM_WIDTH), _bf16),
        jax.ShapeDtypeStruct((B, S, QK_W), _bf16),
        jax.ShapeDtypeStruct((B, S, QK_W), _bf16),
        jax.ShapeDtypeStruct((B, MLA_HEADS * MLA_V, S), _bf16),
        jax.ShapeDtypeStruct((B, S, LRU_WIDTH), _f32),
        jax.ShapeDtypeStruct((B, S, LRU_WIDTH), _bf16),
        jax.ShapeDtypeStruct((B, S, N_BRANCH * D_MODEL), _bf16),
    ]
    return pl.pallas_call(
        _in_proj_kernel,
        out_shape=outs,
        grid=(B, S // TM),
        in_specs=[
            tile(D),
            _mod_spec(l),
            _layer_spec(l, (D, OFF_KROPE)), _layer_spec(l, (1, OFF_KROPE)),
            _layer_spec(l, (D, 2 * HEAD_PAD)), _layer_spec(l, (1, 2 * HEAD_PAD)),
            _layer_spec(l, (D, N_IN - OFF_LRU_X)), _layer_spec(l, (1, N_IN - OFF_LRU_X)),
            _layer_spec(l, (1, GM_WIDTH)),
            _layer_spec(l, (1, GM_WIDTH)),
            _layer_spec(l, (GM_GROUPS, GM_BLOCK, GM_BLOCK)),
            _layer_spec(l, (GM_BLOCK, GM_GROUPS)),
            _layer_spec(l, (1, MLA_Q_RANK)),
            _layer_spec(l, (MLA_Q_RANK, 2 * QK_W)),
            _layer_spec(l, (1, MLA_KV_RANK)),
            _layer_spec(l, (MLA_KV_RANK, QK_W)),
            _layer_spec(l, (MLA_HEADS * MLA_V, MLA_KV_RANK)),
            pl.BlockSpec((TM, HEAD_PAD), lambda b, i: (i, 0)),
            pl.BlockSpec((TM, HEAD_PAD), lambda b, i: (i, 0)),
        ],
        out_specs=[
            tile(GM_WIDTH), tile(QK_W), tile(QK_W),
            pl.BlockSpec((1, MLA_HEADS * MLA_V, TM), lambda b, i: (b, 0, i)),
            tile(LRU_WIDTH), tile(LRU_WIDTH), tile(N_BRANCH * D_MODEL),
        ],
        compiler_params=pltpu.CompilerParams(
            dimension_semantics=("arbitrary", "arbitrary"), vmem_limit_bytes=VMEM_LIMIT),
        name="in_proj",
    )(x, mod, p["w_a"], p["b_a"], p["w_kr"], p["b_kr"], p["w_b"], p["b_b"],
      p["gm_g"], p["gm_b"], p["gm_ws"], p["gm_bst"],
      p["q_g"], p["wuq"], p["kv_g"], p["wuk"], p["wuvt"], cos_t, sin_t)


SUB = 256
ACC_ROWS = MLA_V + 16
LAG_STATS, LAG_PV, LAG_ACC = 3, 5, 8


def _attn_units(rel):
    units = []
    for kh in range(TK // SUB):
        for hd in range(MLA_HEADS):
            for c in range(TQ // SUB):
                mode = "full"
                if rel is not None:
                    koff, qoff = rel + kh * SUB, c * SUB
                    if koff >= qoff + SUB:
                        continue
                    if koff + SUB > qoff:
                        assert koff == qoff
                        mode = "diag"
                units.append((kh, hd, c, mode))
    return units


def _attn_kernel(qi_ref, ki_ref, q_ref, k_ref, vt_ref, o_ref, m_sc, acc_sc, o_sc):
    step = pl.program_id(1)
    qi = qi_ref[step]
    ki = ki_ref[step]
    nsub = SUB // 8

    @pl.when(ki == 0)
    def _():
        m_sc[...] = jnp.full(m_sc.shape, NEG_BIG, _f32)
        acc_sc[...] = jnp.zeros(acc_sc.shape, _f32)

    def stage_qk(unit, st, visible):
        kh, hd, c, mode = unit
        hs = slice(hd * HEAD_PAD, (hd + 1) * HEAD_PAD)
        s = _dot_nt(k_ref[0, kh * SUB:(kh + 1) * SUB, hs], q_ref[0, c * SUB:(c + 1) * SUB, hs])
        if mode == "diag":
            s = jnp.where(visible, s, NEG_BIG)
        st["s3"] = s.reshape(nsub, 8, SUB)

    def stage_stats(unit, st, visible):
        kh, hd, c, mode = unit
        cm = jnp.max(st["s3"], axis=0)
        for d in (4, 2, 1):
            cm = jnp.maximum(cm, pltpu.roll(cm, d, 0))
        msl = (slice(hd * 8, (hd + 1) * 8), slice(c * SUB, (c + 1) * SUB))
        m_old = m_sc[msl]
        m_new = jnp.maximum(m_old, cm)
        m_sc[msl] = m_new
        st["alpha"] = jnp.exp2(m_old - m_new)
        st["p"] = jnp.exp2(st.pop("s3") - m_new[None]).reshape(SUB, SUB).astype(_bf16)

    def stage_pv(unit, st, visible):
        kh, hd, c, _ = unit
        v_t = vt_ref[0, hd * MLA_V:(hd + 1) * MLA_V, kh * SUB:(kh + 1) * SUB]
        v_ext = jnp.concatenate([v_t, jnp.ones((ACC_ROWS - MLA_V, SUB), _bf16)], axis=0)
        st["pv"] = _dot(v_ext, st.pop("p")).reshape(ACC_ROWS // 8, 8, SUB)

    def stage_acc(unit, st, visible):
        kh, hd, c, _ = unit
        cs = slice(c * SUB, (c + 1) * SUB)
        acc = acc_sc[hd, :, cs].reshape(ACC_ROWS // 8, 8, SUB)
        acc_sc[hd, :, cs] = (st.pop("alpha")[None] * acc + st.pop("pv")).reshape(ACC_ROWS, SUB)

    stages = ((stage_qk, 0), (stage_stats, LAG_STATS), (stage_pv, LAG_PV), (stage_acc, LAG_ACC))

    def update(rel):
        units = _attn_units(rel)
        visible = None
        if rel is not None:
            row = lax.broadcasted_iota(jnp.int32, (SUB, SUB), 0) // CHUNK
            col = lax.broadcasted_iota(jnp.int32, (SUB, SUB), 1) // CHUNK
            visible = row <= col
        n = len(units)
        state = [{} for _ in units]
        for t in range(n + LAG_ACC):
            for fn, lag in stages:
                if 0 <= t - lag < n:
                    fn(units[t - lag], state[t - lag], visible)

    rel = ki * TK - qi * TQ
    for r in range(0, TQ, TK):
        pl.when(rel == r)(functools.partial(update, r))
    pl.when(rel < 0)(functools.partial(update, None))

    @pl.when(rel == TQ - TK)
    def _():
        for hd in range(MLA_HEADS):
            inv = 1.0 / acc_sc[hd, MLA_V:MLA_V + 8, :]
            o = acc_sc[hd, 0:MLA_V, :].reshape(MLA_V // 8, 8, TQ) * inv[None]
            o_sc[hd * MLA_V:(hd + 1) * MLA_V, :] = o.reshape(MLA_V, TQ)
        o_ref[0] = o_sc[...].T.astype(_bf16)


def _attention(q, k, vt):
    B, S, _ = q.shape
    nq = S // TQ
    ratio = TQ // TK
    qi_tab, ki_tab = [], []
    for i in range(nq):
        for j in range((i + 1) * ratio):
            qi_tab.append(i)
            ki_tab.append(j)
    n_steps = len(qi_tab)
    qi_tab = jnp.asarray(np.array(qi_tab, np.int32))
    ki_tab = jnp.asarray(np.array(ki_tab, np.int32))
    grid_spec = pltpu.PrefetchScalarGridSpec(
        num_scalar_prefetch=2,
        grid=(B, n_steps),
        in_specs=[
            pl.BlockSpec((1, TQ, QK_W), lambda b, s, qi, ki: (b, qi[s], 0)),
            pl.BlockSpec((1, TK, QK_W), lambda b, s, qi, ki: (b, ki[s], 0)),
            pl.BlockSpec((1, MLA_HEADS * MLA_V, TK), lambda b, s, qi, ki: (b, 0, ki[s])),
        ],
        out_specs=pl.BlockSpec((1, TQ, MLA_HEADS * MLA_V), lambda b, s, qi, ki: (b, qi[s], 0)),
        scratch_shapes=[
            pltpu.VMEM((MLA_HEADS * 8, TQ), _f32),
            pltpu.VMEM((MLA_HEADS, ACC_ROWS, TQ), _f32),
            pltpu.VMEM((MLA_HEADS * MLA_V, TQ), _f32),
        ],
    )
    return pl.pallas_call(
        _attn_kernel,
        out_shape=jax.ShapeDtypeStruct((B, S, MLA_HEADS * MLA_V), _bf16),
        grid_spec=grid_spec,
        compiler_params=pltpu.CompilerParams(
            dimension_semantics=("arbitrary", "arbitrary"), vmem_limit_bytes=VMEM_LIMIT),
        name="mla_attn",
    )(qi_tab, ki_tab, q, k, vt)


SEG = TS // 8
SEG_PITCH = SEG + 8
LRU_SLABS = LRU_WIDTH // 128


def _rglru_kernel(xb_ref, gg_ref, cw_ref, cb_ref, wr_ref, br_ref, wi_ref, bi_ref, lam_ref,
                  y_ref, ext_sc, a_sc, b_sc, hl_sc, cp_sc, carry_sc):
    i = pl.program_id(1)

    @pl.when(i == 0)
    def _():
        ext_sc[0:8, :] = jnp.zeros((8, LRU_WIDTH), _f32)
        carry_sc[...] = jnp.zeros(carry_sc.shape, _f32)

    x = xb_ref[0]
    ext_sc[8:8 + TS, :] = x
    xc = cb_ref[...] + x * cw_ref[CONV_W - 1:CONV_W, :]
    for j in range(1, CONV_W):
        xc = xc + ext_sc[8 - j:8 - j + TS, :] * cw_ref[CONV_W - 1 - j:CONV_W - j, :]
    ext_sc[0:8, :] = x[TS - 8:, :]

    xcb = xc.astype(_bf16)
    th_r = jnp.tanh(_dot(xcb, wr_ref[...]) + br_ref[...])
    th_i = jnp.tanh(_dot(xcb, wi_ref[...]) + bi_ref[...])
    lam = lam_ref[...]
    softplus = jnp.maximum(-lam, 0.0) + jnp.log1p(jnp.exp(-jnp.abs(lam)))
    half = (0.5 * LRU_C) * softplus
    nla = th_r * half + half
    a = jnp.exp(-nla)
    z = jnp.tanh(nla) * (a * a + 1.0)
    gain = jnp.where(z > 0.0, z * lax.rsqrt(z), 0.0)
    b = gain * ((0.5 * th_i + 0.5) * xc)

    for sl in range(LRU_SLABS):
        ls = slice(sl * 128, (sl + 1) * 128)
        for r in range(8):
            a_sc[sl, r * SEG_PITCH:r * SEG_PITCH + SEG, :] = a[r * SEG:(r + 1) * SEG, ls]
            b_sc[sl, r * SEG_PITCH:r * SEG_PITCH + SEG, :] = b[r * SEG:(r + 1) * SEG, ls]

    h = [None] * LRU_SLABS
    cp = [None] * LRU_SLABS
    for g in range(SEG):
        for sl in range(LRU_SLABS):
            ls = slice(sl * 128, (sl + 1) * 128)
            a_g = a_sc.at[sl][pl.ds(g, 8, stride=SEG_PITCH), :]
            b_g = b_sc.at[sl][pl.ds(g, 8, stride=SEG_PITCH), :]
            h[sl] = b_g if g == 0 else a_g * h[sl] + b_g
            cp[sl] = a_g if g == 0 else a_g * cp[sl]
            hl_sc[g, :, ls] = h[sl]
            cp_sc[g, :, ls] = cp[sl]

    rows = lax.broadcasted_iota(jnp.int32, (8, 128), 0)
    for sl in range(LRU_SLABS):
        ls = slice(sl * 128, (sl + 1) * 128)
        a8 = jnp.where(rows == 0, 0.0, pltpu.roll(cp[sl], 1, 0))
        b8 = jnp.where(rows == 0, carry_sc[:, ls], pltpu.roll(h[sl], 1, 0))
        for d in (1, 2, 4):
            ok = rows >= d
            a_sh = pltpu.roll(a8, d, 0)
            b_sh = pltpu.roll(b8, d, 0)
            b8 = jnp.where(ok, a8 * b_sh + b8, b8)
            a8 = jnp.where(ok, a8 * a_sh, a8)
        inflow = b8
        for g in range(SEG):
            hf = hl_sc[g, :, ls] + cp_sc[g, :, ls] * inflow
            a_sc.at[sl][pl.ds(g, 8, stride=SEG_PITCH), :] = hf
        carry_sc[:, ls] = jnp.broadcast_to(hf[7:8, :], (8, 128))
        for r in range(8):
            rs = slice(r * SEG, (r + 1) * SEG)
            hr = a_sc[sl, r * SEG_PITCH:r * SEG_PITCH + SEG, :]
            y_ref[0, rs, ls] = (hr * gg_ref[0, rs, ls].astype(_f32)).astype(_bf16)


def _rglru(l, xb, gg, p):
    B, S, W = xb.shape
    tile = pl.BlockSpec((1, TS, W), lambda b, i: (b, i, 0))
    return pl.pallas_call(
        _rglru_kernel,
        out_shape=jax.ShapeDtypeStruct((B, S, W), _bf16),
        grid=(B, S // TS),
        in_specs=[
            tile, tile,
            _layer_spec(l, (CONV_W, W)), _layer_spec(l, (1, W)),
            _layer_spec(l, (W, W)), _layer_spec(l, (1, W)),
            _layer_spec(l, (W, W)), _layer_spec(l, (1, W)),
            _layer_spec(l, (1, W)),
        ],
        out_specs=tile,
        scratch_shapes=[
            pltpu.VMEM((TS + 8, W), _f32),
            pltpu.VMEM((LRU_SLABS, 8 * SEG_PITCH, 128), _f32),
            pltpu.VMEM((LRU_SLABS, 8 * SEG_PITCH, 128), _f32),
            pltpu.VMEM((SEG, 8, W), _f32),
            pltpu.VMEM((SEG, 8, W), _f32),
            pltpu.VMEM((8, W), _f32),
        ],
        compiler_params=pltpu.CompilerParams(
            dimension_semantics=("arbitrary", "arbitrary"), vmem_limit_bytes=VMEM_LIMIT),
        name="rglru",
    )(xb, gg, p["conv_w"], p["conv_b"], p["wr"], p["br"], p["wi"], p["bi"], p["lam"])


def _merge_kernel(x_ref, mod_ref, ya_ref, yb_ref, yc_ref, zg_ref, bw_ref, mw_ref, lng_ref,
                  lnb_ref, o_ref):
    merged = None
    for n, y_ref in enumerate((ya_ref, yb_ref, yc_ref)):
        gate = _sigmoid(zg_ref[0, :, n * D_MODEL:(n + 1) * D_MODEL].astype(_f32))
        t = gate * _dot(y_ref[0], bw_ref[n])
        merged = t if merged is None else merged + t
    mix = _dot(merged.astype(_bf16), mw_ref[...])
    r = ALPHA * x_ref[0] + mod_ref[0, 2:3, :] * mix
    o_ref[0] = _ln(r) * lng_ref[...] + lnb_ref[...]


def _merge(l, x, mod, ya, yb, yc, zg, p):
    B, S, D = x.shape
    tile = lambda w: pl.BlockSpec((1, TM, w), lambda b, i: (b, i, 0))
    return pl.pallas_call(
        _merge_kernel,
        out_shape=jax.ShapeDtypeStruct((B, S, D), _f32),
        grid=(B, S // TM),
        in_specs=[
            tile(D),
            _mod_spec(l),
            tile(GM_WIDTH), tile(GM_WIDTH), tile(GM_WIDTH), tile(N_BRANCH * D),
            _layer_spec(l, (N_BRANCH, GM_WIDTH, D)),
            _layer_spec(l, (D, D)),
            _layer_spec(l, (1, D)), _layer_spec(l, (1, D)),
        ],
        out_specs=tile(D),
        compiler_params=pltpu.CompilerParams(
            dimension_semantics=("arbitrary", "arbitrary"), vmem_limit_bytes=VMEM_LIMIT),
        name="merge",
    )(x, mod, ya, yb, yc, zg, p["branch_w"], p["mix_w"], p["ln_g0"], p["ln_b0"])


FF_CHUNK = 1024


FF_SUB = 256


def _ffn_kernel(x_ref, mod_ref, w1_ref, b1_ref, w2_ref, b2_ref, lng_ref, lnb_ref, o_ref):
    nsub = TM // FF_SUB
    nchunk = D_FF // FF_CHUNK
    hs = [None] * nsub
    fs = [None] * nsub

    def rows(j):
        return slice(j * FF_SUB, (j + 1) * FF_SUB)

    def pre(j):
        hs[j] = (_ln(x_ref[0, rows(j), :]) * mod_ref[0, 4:5, :] + mod_ref[0, 3:4, :]).astype(_bf16)

    def mm(j, c):
        cs = slice(c * FF_CHUNK, (c + 1) * FF_CHUNK)
        a = jnp.maximum(_dot(hs[j], w1_ref[:, cs]) + b1_ref[:, cs], 0.0)
        t = _dot((a * a).astype(_bf16), w2_ref[cs, :])
        fs[j] = t if c == 0 else fs[j] + t

    def post(j):
        r = ALPHA * x_ref[0, rows(j), :] + mod_ref[0, 5:6, :] * (fs[j] + b2_ref[...])
        o_ref[0, rows(j), :] = _ln(r) * lng_ref[...] + lnb_ref[...]

    pre(0)
    for j in range(nsub):
        for c in range(nchunk):
            mm(j, c)
            if c == 0 and j + 1 < nsub:
                pre(j + 1)
            if c == 1 and j > 0:
                post(j - 1)
    post(nsub - 1)


def _ffn(l, x, mod, p):
    B, S, D = x.shape
    tile = pl.BlockSpec((1, TM, D), lambda b, i: (b, i, 0))
    return pl.pallas_call(
        _ffn_kernel,
        out_shape=jax.ShapeDtypeStruct((B, S, D), _f32),
        grid=(B, S // TM),
        in_specs=[
            tile,
            _mod_spec(l),
            _layer_spec(l, (D, D_FF)), _layer_spec(l, (1, D_FF)),
            _layer_spec(l, (D_FF, D)), _layer_spec(l, (1, D)),
            _layer_spec(l, (1, D)), _layer_spec(l, (1, D)),
        ],
        out_specs=tile,
        compiler_params=pltpu.CompilerParams(
            dimension_semantics=("arbitrary", "arbitrary"), vmem_limit_bytes=VMEM_LIMIT),
        name="ffn",
    )(x, mod, p["w1"], p["b1"], p["w2"], p["b2"], p["ln_g1"], p["ln_b1"])


def _swap_halves(w):
    half = w.shape[-1] // 2
    return jnp.concatenate([w[..., half:], w[..., :half]], axis=-1)


def _head_pad(nope, rope):
    parts = [nope, rope, jnp.zeros(nope.shape[:-1] + (HEAD_PAD - MLA_NOPE - MLA_ROPE,), nope.dtype)]
    out = jnp.concatenate(parts, axis=-1)
    return out.reshape(out.shape[:-2] + (MLA_HEADS * HEAD_PAD,))


def _block_diag(w):
    n, c, d = w.shape[-3:]
    eye = jnp.eye(n, dtype=w.dtype)
    dense = w[..., :, :, None, :] * eye[:, None, :, None]
    return dense.reshape(w.shape[:-3] + (n * c, n * d))


def _prepare_params(in_w, in_b, gm_ln_g, gm_ln_b, gm_ws, gm_bs, mla_qnorm_g, mla_wuq,
                    mla_kvnorm_g, mla_wukv, lru_conv_w, lru_conv_b, lru_wr, lru_br, lru_wi,
                    lru_bi, lru_lambda, branch_w, mix_out_w, ffn_w1, ffn_b1, ffn_w2, ffn_b2,
                    ln_g, ln_b):
    L = in_w.shape[0]
    in_b = in_b[:, None, :]

    def rope_cols(t):
        kr = t[..., OFF_KROPE:OFF_LRU_X]
        z64 = jnp.zeros(kr.shape[:-1] + (MLA_NOPE,), t.dtype)
        z32 = jnp.zeros(kr.shape[:-1] + (HEAD_PAD - MLA_NOPE - MLA_ROPE,), t.dtype)
        return jnp.concatenate([z64, kr, z32, z64, _swap_halves(kr), z32], axis=-1)

    wuq = mla_wuq.reshape(L, MLA_Q_RANK, MLA_HEADS, MLA_NOPE + MLA_ROPE)
    zq = jnp.zeros((L, MLA_Q_RANK, MLA_HEADS, MLA_NOPE), _f32)
    wuq_main = _head_pad(wuq[..., :MLA_NOPE], wuq[..., MLA_NOPE:])
    wuq_swap = _head_pad(zq, _swap_halves(wuq[..., MLA_NOPE:]))
    wukv = mla_wukv.reshape(L, MLA_KV_RANK, MLA_HEADS, MLA_NOPE + MLA_V)
    wuk = _head_pad(wukv[..., :MLA_NOPE], jnp.zeros((L, MLA_KV_RANK, MLA_HEADS, MLA_ROPE), _f32))
    wuvt = wukv[..., MLA_NOPE:].reshape(L, MLA_KV_RANK, MLA_HEADS * MLA_V).transpose(0, 2, 1)
    row = lambda v: v.reshape(L, 1, -1)
    return dict(
        w_a=in_w[:, :, :OFF_KROPE].astype(_bf16), b_a=in_b[:, :, :OFF_KROPE],
        w_kr=rope_cols(in_w).astype(_bf16), b_kr=rope_cols(in_b),
        w_b=in_w[:, :, OFF_LRU_X:].astype(_bf16), b_b=in_b[:, :, OFF_LRU_X:],
        gm_g=row(gm_ln_g), gm_b=row(gm_ln_b),
        gm_ws=gm_ws, gm_bst=gm_bs.transpose(0, 2, 1),
        q_g=row(mla_qnorm_g),
        wuq=jnp.concatenate([wuq_main, wuq_swap], axis=-1).astype(_bf16),
        kv_g=row(mla_kvnorm_g),
        wuk=wuk.astype(_bf16), wuvt=wuvt.astype(_bf16),
        conv_w=lru_conv_w, conv_b=row(lru_conv_b),
        wr=(0.5 * _block_diag(lru_wr)).astype(_bf16), br=0.5 * row(lru_br),
        wi=(0.5 * _block_diag(lru_wi)).astype(_bf16), bi=0.5 * row(lru_bi),
        lam=row(lru_lambda),
        branch_w=branch_w.astype(_bf16), mix_w=mix_out_w.astype(_bf16),
        w1=ffn_w1.astype(_bf16), b1=row(ffn_b1),
        w2=ffn_w2.astype(_bf16), b2=row(ffn_b2),
        ln_g0=row(ln_g[:, 0]), ln_b0=row(ln_b[:, 0]),
        ln_g1=row(ln_g[:, 1]), ln_b1=row(ln_b[:, 1]),
    )


def _rope_tables(seq):
    pos = jnp.arange(seq, dtype=_f32)
    inv = ROPE_BASE ** (-jnp.arange(0, MLA_ROPE, 2, dtype=_f32) / MLA_ROPE)
    ang = pos[:, None] * inv[None, :]
    cos, sin = jnp.cos(ang), jnp.sin(ang)
    ones = jnp.ones((seq, MLA_NOPE), _f32)
    zeros = jnp.zeros((seq, MLA_NOPE), _f32)
    pad = jnp.zeros((seq, HEAD_PAD - MLA_NOPE - MLA_ROPE), _f32)
    cos_t = jnp.concatenate([ones, cos, cos, pad], axis=1)
    sin_t = jnp.concatenate([zeros, -sin, sin, pad], axis=1)
    return cos_t, sin_t


def kernel(x, c, ada_w, ada_b, in_w, in_b, gm_ln_g, gm_ln_b, gm_ws, gm_bs, mla_qnorm_g, mla_wuq, mla_kvnorm_g, mla_wukv, lru_conv_w, lru_conv_b, lru_wr, lru_br, lru_wi, lru_bi, lru_lambda, branch_w, mix_out_w, ffn_w1, ffn_b1, ffn_w2, ffn_b2, ln_g, ln_b):
    B, S, D = x.shape
    assert (B, S, D) == (c.shape[0], S, D_MODEL) and S % TQ == 0 and S % TM == 0
    cos_t, sin_t = _rope_tables(S)
    p = _prepare_params(in_w, in_b, gm_ln_g, gm_ln_b, gm_ws, gm_bs, mla_qnorm_g, mla_wuq,
                        mla_kvnorm_g, mla_wukv, lru_conv_w, lru_conv_b, lru_wr, lru_br, lru_wi,
                        lru_bi, lru_lambda, branch_w, mix_out_w, ffn_w1, ffn_b1, ffn_w2, ffn_b2,
                        ln_g, ln_b)
    c8 = jnp.concatenate([c, jnp.zeros((8 - B, D), c.dtype)], axis=0)
    mod_all = _modulation(c8, ada_w, ada_b.reshape(DEPTH, 1, 6 * D))
    m6 = mod_all[:, :B].reshape(DEPTH, B, 6, D)
    m6 = m6 + jnp.array([0.0, 1.0, 1.0, 0.0, 1.0, 1.0], _f32).reshape(1, 1, 6, 1)
    mod = jnp.concatenate([m6, jnp.zeros((DEPTH, B, 2, D), _f32)], axis=2)
    for l in range(DEPTH):
        ya, q, k, vt, xb, gg, zg = _in_proj(l, x, mod, p, cos_t, sin_t)
        yb = _attention(q, k, vt)
        yc = _rglru(l, xb, gg, p)
        x = _merge(l, x, mod, ya, yb, yc, zg, p)
        x = _ffn(l, x, mod, p)
    return x
```

```python
import functools
import math

import jax
import jax.numpy as jnp
import numpy as np
from jax import lax
from jax.experimental import pallas as pl
from jax.experimental.pallas import tpu as pltpu

D_MODEL = 1024
DEPTH = 2
CHUNK = 64
GM_BLOCK = 128
GM_WIDTH = 512
GM_GROUPS = 4
MLA_HEADS = 8
MLA_Q_RANK = 256
MLA_KV_RANK = 128
MLA_NOPE = 64
MLA_ROPE = 32
MLA_V = 64
ROPE_BASE = 10000.0
LRU_WIDTH = 512
LRU_BLOCKS = 8
CONV_W = 4
LRU_C = 8.0
N_BRANCH = 3
D_FF = 4 * D_MODEL
ALPHA = (2.0 * DEPTH) ** 0.25
LN_EPS = 1e-5
RMS_EPS = 1e-6

OFF_QLAT = 2 * GM_WIDTH
OFF_KVLAT = OFF_QLAT + MLA_Q_RANK
OFF_KROPE = OFF_KVLAT + MLA_KV_RANK
OFF_LRU_X = OFF_KROPE + MLA_ROPE
OFF_LRU_G = OFF_LRU_X + LRU_WIDTH
OFF_GATE = OFF_LRU_G + LRU_WIDTH
N_IN = OFF_GATE + N_BRANCH * D_MODEL

HEAD_PAD = 128
QK_W = MLA_HEADS * HEAD_PAD
OFF_GM = 0
LAT_W = MLA_Q_RANK + MLA_KV_RANK + HEAD_PAD

TM = 512
TQ = 1024
TK = 512
TS = 512
VMEM_LIMIT = 56 * 1024 * 1024
NEG_BIG = -1e30
LOG2E = math.log2(math.e)

_bf16 = jnp.bfloat16
_f32 = jnp.float32


def _ln(x):
    mu = jnp.mean(x, axis=-1, keepdims=True)
    xc = x - mu
    var = jnp.mean(xc * xc, axis=-1, keepdims=True)
    return xc * lax.rsqrt(var + LN_EPS)


def _rms(x, g):
    return x * lax.rsqrt(jnp.mean(x * x, axis=-1, keepdims=True) + RMS_EPS) * g


def _sigmoid(x):
    return 0.5 * jnp.tanh(0.5 * x) + 0.5


def _dot(a, b):
    return jnp.dot(a, b, preferred_element_type=_f32)


def _dot_nt(a, b):
    return lax.dot_general(a, b, (((1,), (1,)), ((), ())), preferred_element_type=_f32)


def _mod_spec(l):
    return pl.BlockSpec((None, 1, 8, D_MODEL), lambda b, i: (l, b, 0, 0))


def _layer_spec(l, shape):
    nd = len(shape)
    return pl.BlockSpec((None,) + tuple(shape), lambda *_: (l,) + (0,) * nd,
                        pipeline_mode=pl.Buffered(1))


def _mod_kernel(c_ref, w_ref, b_ref, o_ref):
    c = c_ref[...]
    c_act = (c * jax.nn.sigmoid(c)).astype(_bf16)
    o_ref[0] = _dot(c_act, w_ref[0].astype(_bf16)) + b_ref[0]


def _modulation(c8, ada_w, ada_b):
    L = ada_w.shape[0]
    nb = 1536
    return pl.pallas_call(
        _mod_kernel,
        out_shape=jax.ShapeDtypeStruct((L, 8, 6 * D_MODEL), _f32),
        grid=(L, 6 * D_MODEL // nb),
        in_specs=[
            pl.BlockSpec((8, D_MODEL), lambda l, j: (0, 0)),
            pl.BlockSpec((1, D_MODEL, nb), lambda l, j: (l, 0, j)),
            pl.BlockSpec((1, 1, nb), lambda l, j: (l, 0, j)),
        ],
        out_specs=pl.BlockSpec((1, 8, nb), lambda l, j: (l, 0, j)),
        compiler_params=pltpu.CompilerParams(
            dimension_semantics=("arbitrary", "arbitrary"), vmem_limit_bytes=VMEM_LIMIT),
        name="adaln_mod",
    )(c8, ada_w, ada_b)


def _in_proj_kernel(x_ref, mod_ref, wa_ref, ba_ref, wlat_ref, blat_ref, wb_ref, bb_ref,
                    gmg_ref, gmb_ref, ws_ref, bst_ref,
                    qg_ref, wuqt_ref, kvg_ref, wuk_ref, wuvt_ref, cosk_ref, sin_ref, cosqt_ref, sint_ref,
                    ya_ref, q_ref, k_ref, vt_ref, xb_ref, gg_ref, zg_ref):
    x = x_ref[0]
    h = (_ln(x) * mod_ref[0, 1:2, :] + mod_ref[0, 0:1, :]).astype(_bf16)

    def proj(w_ref, b_ref, lo, hi):
        return _dot(h, w_ref[:, lo:hi]) + b_ref[:, lo:hi]

    zg = jax.nn.gelu(proj(wa_ref, ba_ref, OFF_GM, OFF_QLAT))
    u = zg[:, :GM_WIDTH]
    v = (_ln(zg[:, GM_WIDTH:]) * gmg_ref[...] + gmb_ref[...]).astype(_bf16)
    row = lax.broadcasted_iota(jnp.int32, (GM_BLOCK, GM_BLOCK), 0) // CHUNK
    col = lax.broadcasted_iota(jnp.int32, (GM_BLOCK, GM_BLOCK), 1) // CHUNK
    gch = GM_WIDTH // GM_GROUPS
    for g in range(GM_GROUPS):
        wg = jnp.where(row >= col, ws_ref[g], 0.0).astype(_bf16)
        bias = bst_ref[:, g:g + 1]
        for r in range(TM // GM_BLOCK):
            rs = slice(r * GM_BLOCK, (r + 1) * GM_BLOCK)
            cs = slice(g * gch, (g + 1) * gch)
            f = _dot(wg, v[rs, cs]) + bias
            ya_ref[0, rs, cs] = (u[rs, cs] * f).astype(_bf16)

    zlat = proj(wlat_ref, blat_ref, 0, LAT_W)
    qn = _rms(zlat[:, :MLA_Q_RANK], qg_ref[...]).astype(_bf16)
    qscale = (MLA_NOPE + MLA_ROPE) ** -0.5 * LOG2E
    qmt = _dot_nt(wuqt_ref[...], qn)
    cos_t = cosqt_ref[...]
    sin_t = sint_ref[...]
    for hd in range(MLA_HEADS):
        hs = slice(hd * HEAD_PAD, (hd + 1) * HEAD_PAD)
        g = qmt[hs, :]
        rot = pltpu.roll(g, HEAD_PAD - MLA_ROPE, 0)
        q_ref[0, hs, :] = ((g * cos_t + rot * sin_t) * qscale).astype(_bf16)

    ckv = _rms(zlat[:, MLA_Q_RANK:MLA_Q_RANK + MLA_KV_RANK], kvg_ref[...]).astype(_bf16)
    gk = zlat[:, MLA_Q_RANK + MLA_KV_RANK:]
    kr = gk * cosk_ref[...] + pltpu.roll(gk, MLA_NOPE, 1) * sin_ref[...]
    kn = _dot(ckv, wuk_ref[...])
    for hd in range(MLA_HEADS):
        hs = slice(hd * HEAD_PAD, (hd + 1) * HEAD_PAD)
        k_ref[0, :, hs] = (kn[:, hs] + kr).astype(_bf16)
    vt_ref[0] = _dot_nt(wuvt_ref[...], ckv).astype(_bf16)

    xb_ref[0] = proj(wb_ref, bb_ref, 0, LRU_WIDTH)
    gg_ref[0] = jax.nn.gelu(proj(wb_ref, bb_ref, LRU_WIDTH, 2 * LRU_WIDTH)).astype(_bf16)
    for n in range(N_BRANCH):
        lo = 2 * LRU_WIDTH + n * D_MODEL
        zg_ref[0, :, n * D_MODEL:(n + 1) * D_MODEL] = proj(wb_ref, bb_ref, lo, lo + D_MODEL).astype(_bf16)


def _in_proj(l, x, mod, p, rope):
    B, S, D = x.shape
    tile = lambda w: pl.BlockSpec((1, TM, w), lambda b, i: (b, i, 0))
    outs = [
        jax.ShapeDtypeStruct((B, S, GM_WIDTH), _bf16),
        jax.ShapeDtypeStruct((B, QK_W, S), _bf16),
        jax.ShapeDtypeStruct((B, S, QK_W), _bf16),
        jax.ShapeDtypeStruct((B, MLA_HEADS * MLA_V, S), _bf16),
        jax.ShapeDtypeStruct((B, S, LRU_WIDTH), _f32),
        jax.ShapeDtypeStruct((B, S, LRU_WIDTH), _bf16),
        jax.ShapeDtypeStruct((B, S, N_BRANCH * D_MODEL), _bf16),
    ]
    return pl.pallas_call(
        _in_proj_kernel,
        out_shape=outs,
        grid=(B, S // TM),
        in_specs=[
            tile(D),
            _mod_spec(l),
            _layer_spec(l, (D, OFF_QLAT)), _layer_spec(l, (1, OFF_QLAT)),
            _layer_spec(l, (D, LAT_W)), _layer_spec(l, (1, LAT_W)),
            _layer_spec(l, (D, N_IN - OFF_LRU_X)), _layer_spec(l, (1, N_IN - OFF_LRU_X)),
            _layer_spec(l, (1, GM_WIDTH)),
            _layer_spec(l, (1, GM_WIDTH)),
            _layer_spec(l, (GM_GROUPS, GM_BLOCK, GM_BLOCK)),
            _layer_spec(l, (GM_BLOCK, GM_GROUPS)),
            _layer_spec(l, (1, MLA_Q_RANK)),
            _layer_spec(l, (QK_W, MLA_Q_RANK)),
            _layer_spec(l, (1, MLA_KV_RANK)),
            _layer_spec(l, (MLA_KV_RANK, QK_W)),
            _layer_spec(l, (MLA_HEADS * MLA_V, MLA_KV_RANK)),
            pl.BlockSpec((TM, HEAD_PAD), lambda b, i: (i, 0)),
            pl.BlockSpec((TM, HEAD_PAD), lambda b, i: (i, 0)),
            pl.BlockSpec((HEAD_PAD, TM), lambda b, i: (0, i)),
            pl.BlockSpec((HEAD_PAD, TM), lambda b, i: (0, i)),
        ],
        out_specs=[
            tile(GM_WIDTH),
            pl.BlockSpec((1, QK_W, TM), lambda b, i: (b, 0, i)),
            tile(QK_W),
            pl.BlockSpec((1, MLA_HEADS * MLA_V, TM), lambda b, i: (b, 0, i)),
            tile(LRU_WIDTH), tile(LRU_WIDTH), tile(N_BRANCH * D_MODEL),
        ],
        compiler_params=pltpu.CompilerParams(
            dimension_semantics=("arbitrary", "arbitrary"), vmem_limit_bytes=VMEM_LIMIT),
        name="in_proj",
    )(x, mod, p["w_a"], p["b_a"], p["w_lat"], p["b_lat"], p["w_b"], p["b_b"],
      p["gm_g"], p["gm_b"], p["gm_ws"], p["gm_bst"],
      p["q_g"], p["wuqt"], p["kv_g"], p["wuk"], p["wuvt"], *rope)


SUB = 256
ACC_ROWS = MLA_V + 16
LAG_STATS, LAG_PV, LAG_ACC = 3, 5, 8


def _attn_units(rel):
    units = []
    for kh in range(TK // SUB):
        for hd in range(MLA_HEADS):
            for c in range(TQ // SUB):
                mode = "full"
                if rel is not None:
                    koff, qoff = rel + kh * SUB, c * SUB
                    if koff >= qoff + SUB:
                        continue
                    if koff + SUB > qoff:
                        assert koff == qoff
                        mode = "diag"
                units.append((kh, hd, c, mode))
    return units


def _attn_kernel(qi_ref, ki_ref, q_ref, k_ref, vt_ref, o_ref, m_sc, acc_sc, o_sc):
    step = pl.program_id(1)
    qi = qi_ref[step]
    ki = ki_ref[step]
    nsub = SUB // 8

    @pl.when(ki == 0)
    def _():
        m_sc[...] = jnp.full(m_sc.shape, NEG_BIG, _f32)
        acc_sc[...] = jnp.zeros(acc_sc.shape, _f32)

    def stage_qk(unit, st, visible):
        kh, hd, c, mode = unit
        hs = slice(hd * HEAD_PAD, (hd + 1) * HEAD_PAD)
        s = _dot(k_ref[0, kh * SUB:(kh + 1) * SUB, hs], q_ref[0, hs, c * SUB:(c + 1) * SUB])
        if mode == "diag":
            s = jnp.where(visible, s, NEG_BIG)
        st["s3"] = s.reshape(nsub, 8, SUB)

    def stage_stats(unit, st, visible):
        kh, hd, c, mode = unit
        cm = jnp.max(st["s3"], axis=0)
        for d in (4, 2, 1):
            cm = jnp.maximum(cm, pltpu.roll(cm, d, 0))
        msl = (slice(hd * 8, (hd + 1) * 8), slice(c * SUB, (c + 1) * SUB))
        m_old = m_sc[msl]
        m_new = jnp.maximum(m_old, cm)
        m_sc[msl] = m_new
        st["alpha"] = jnp.exp2(m_old - m_new)
        st["p"] = jnp.exp2(st.pop("s3") - m_new[None]).reshape(SUB, SUB).astype(_bf16)

    def stage_pv(unit, st, visible):
        kh, hd, c, _ = unit
        v_t = vt_ref[0, hd * MLA_V:(hd + 1) * MLA_V, kh * SUB:(kh + 1) * SUB]
        v_ext = jnp.concatenate([v_t, jnp.ones((ACC_ROWS - MLA_V, SUB), _bf16)], axis=0)
        st["pv"] = _dot(v_ext, st.pop("p")).reshape(ACC_ROWS // 8, 8, SUB)

    def stage_acc(unit, st, visible):
        kh, hd, c, _ = unit
        cs = slice(c * SUB, (c + 1) * SUB)
        acc = acc_sc[hd, :, cs].reshape(ACC_ROWS // 8, 8, SUB)
        acc_sc[hd, :, cs] = (st.pop("alpha")[None] * acc + st.pop("pv")).reshape(ACC_ROWS, SUB)

    stages = ((stage_qk, 0), (stage_stats, LAG_STATS), (stage_pv, LAG_PV), (stage_acc, LAG_ACC))

    def update(rel):
        units = _attn_units(rel)
        visible = None
        if rel is not None:
            row = lax.broadcasted_iota(jnp.int32, (SUB, SUB), 0) // CHUNK
            col = lax.broadcasted_iota(jnp.int32, (SUB, SUB), 1) // CHUNK
            visible = row <= col
        n = len(units)
        state = [{} for _ in units]
        for t in range(n + LAG_ACC):
            for fn, lag in stages:
                if 0 <= t - lag < n:
                    fn(units[t - lag], state[t - lag], visible)

    rel = ki * TK - qi * TQ
    for r in range(0, TQ, TK):
        pl.when(rel == r)(functools.partial(update, r))
    pl.when(rel < 0)(functools.partial(update, None))

    @pl.when(rel == TQ - TK)
    def _():
        for hd in range(MLA_HEADS):
            inv = 1.0 / acc_sc[hd, MLA_V:MLA_V + 8, :]
            o = acc_sc[hd, 0:MLA_V, :].reshape(MLA_V // 8, 8, TQ) * inv[None]
            o_sc[hd * MLA_V:(hd + 1) * MLA_V, :] = o.reshape(MLA_V, TQ)
        o_ref[0] = o_sc[...].T.astype(_bf16)


def _attention(qt, k, vt):
    B, _, S = qt.shape
    nq = S // TQ
    ratio = TQ // TK
    qi_tab, ki_tab = [], []
    for i in range(nq):
        for j in range((i + 1) * ratio):
            qi_tab.append(i)
            ki_tab.append(j)
    n_steps = len(qi_tab)
    qi_tab = jnp.asarray(np.array(qi_tab, np.int32))
    ki_tab = jnp.asarray(np.array(ki_tab, np.int32))
    grid_spec = pltpu.PrefetchScalarGridSpec(
        num_scalar_prefetch=2,
        grid=(B, n_steps),
        in_specs=[
            pl.BlockSpec((1, QK_W, TQ), lambda b, s, qi, ki: (b, 0, qi[s])),
            pl.BlockSpec((1, TK, QK_W), lambda b, s, qi, ki: (b, ki[s], 0)),
            pl.BlockSpec((1, MLA_HEADS * MLA_V, TK), lambda b, s, qi, ki: (b, 0, ki[s])),
        ],
        out_specs=pl.BlockSpec((1, TQ, MLA_HEADS * MLA_V), lambda b, s, qi, ki: (b, qi[s], 0)),
        scratch_shapes=[
            pltpu.VMEM((MLA_HEADS * 8, TQ), _f32),
            pltpu.VMEM((MLA_HEADS, ACC_ROWS, TQ), _f32),
            pltpu.VMEM((MLA_HEADS * MLA_V, TQ), _f32),
        ],
    )
    return pl.pallas_call(
        _attn_kernel,
        out_shape=jax.ShapeDtypeStruct((B, S, MLA_HEADS * MLA_V), _bf16),
        grid_spec=grid_spec,
        compiler_params=pltpu.CompilerParams(
            dimension_semantics=("arbitrary", "arbitrary"), vmem_limit_bytes=VMEM_LIMIT),
        name="mla_attn",
    )(qi_tab, ki_tab, qt, k, vt)


SEG = TS // 8
SEG_PITCH = SEG + 8
LRU_SLABS = LRU_WIDTH // 128


def _rglru_kernel(xb_ref, gg_ref, cw_ref, cb_ref, wr_ref, br_ref, wi_ref, bi_ref, lam_ref,
                  y_ref, seg_sc, hl_sc, cp_sc, tail_sc, carry_sc):
    i = pl.program_id(1)

    @pl.when(i == 0)
    def _():
        tail_sc[...] = jnp.zeros(tail_sc.shape, _f32)
        carry_sc[...] = jnp.zeros(carry_sc.shape, _f32)

    for sl in range(LRU_SLABS):
        for r in range(8):
            seg_sc[sl, r * SEG_PITCH:r * SEG_PITCH + SEG, :] = (
                xb_ref[0, r * SEG:(r + 1) * SEG, sl * 128:(sl + 1) * 128])

    rows = lax.broadcasted_iota(jnp.int32, (8, 128), 0)
    xc_slabs = []
    for sl in range(LRU_SLABS):
        ls = slice(sl * 128, (sl + 1) * 128)
        xg = [seg_sc.at[sl][pl.ds(g, 8, stride=SEG_PITCH), :] for g in range(SEG)]
        before = [jnp.where(rows == 0, tail_sc[j - 1, :, ls], pltpu.roll(xg[SEG - j], 1, 0))
                  for j in range(1, CONV_W)]
        xs = before[::-1] + xg
        xc_g = []
        for g in range(SEG):
            acc = cb_ref[:, ls] + xs[g + CONV_W - 1] * cw_ref[CONV_W - 1:CONV_W, ls]
            for j in range(1, CONV_W):
                acc = acc + xs[g + CONV_W - 1 - j] * cw_ref[CONV_W - 1 - j:CONV_W - j, ls]
            xc_g.append(acc)
        for j in range(1, CONV_W):
            tail_sc[j - 1, :, ls] = jnp.broadcast_to(xg[SEG - j][7:8, :], (8, 128))
        xc_slabs.append(jnp.concatenate(xc_g, axis=0))
    xc = jnp.concatenate(xc_slabs, axis=1)

    xcb = xc.astype(_bf16)
    th_r = jnp.tanh(_dot(xcb, wr_ref[...]) + br_ref[...])
    th_i = jnp.tanh(_dot(xcb, wi_ref[...]) + bi_ref[...])
    lam = lam_ref[...]
    softplus = jnp.maximum(-lam, 0.0) + jnp.log1p(jnp.exp(-jnp.abs(lam)))
    half = (0.5 * LRU_C) * softplus
    nla = th_r * half + half
    a = jnp.exp(-nla)
    z = jnp.tanh(nla) * (a * a + 1.0)
    gain = jnp.where(z > 0.0, z * lax.rsqrt(z), 0.0)
    b = gain * ((0.5 * th_i + 0.5) * xc)

    h = [None] * LRU_SLABS
    cp = [None] * LRU_SLABS
    for g in range(SEG):
        for sl in range(LRU_SLABS):
            ls = slice(sl * 128, (sl + 1) * 128)
            a_g = a[g * 8:(g + 1) * 8, ls]
            b_g = b[g * 8:(g + 1) * 8, ls]
            h[sl] = b_g if g == 0 else a_g * h[sl] + b_g
            cp[sl] = a_g if g == 0 else a_g * cp[sl]
            hl_sc[g, :, ls] = h[sl]
            cp_sc[g, :, ls] = cp[sl]

    for sl in range(LRU_SLABS):
        ls = slice(sl * 128, (sl + 1) * 128)
        a8 = jnp.where(rows == 0, 0.0, pltpu.roll(cp[sl], 1, 0))
        b8 = jnp.where(rows == 0, carry_sc[:, ls], pltpu.roll(h[sl], 1, 0))
        for d in (1, 2, 4):
            ok = rows >= d
            a_sh = pltpu.roll(a8, d, 0)
            b_sh = pltpu.roll(b8, d, 0)
            b8 = jnp.where(ok, a8 * b_sh + b8, b8)
            a8 = jnp.where(ok, a8 * a_sh, a8)
        inflow = b8
        for g in range(SEG):
            hf = hl_sc[g, :, ls] + cp_sc[g, :, ls] * inflow
            seg_sc.at[sl][pl.ds(g, 8, stride=SEG_PITCH), :] = hf
        carry_sc[:, ls] = jnp.broadcast_to(hf[7:8, :], (8, 128))
        for r in range(8):
            rs = slice(r * SEG, (r + 1) * SEG)
            hr = seg_sc[sl, r * SEG_PITCH:r * SEG_PITCH + SEG, :]
            y_ref[0, rs, ls] = (hr * gg_ref[0, rs, ls].astype(_f32)).astype(_bf16)


def _rglru(l, xb, gg, p):
    B, S, W = xb.shape
    tile = pl.BlockSpec((1, TS, W), lambda b, i: (b, i, 0))
    return pl.pallas_call(
        _rglru_kernel,
        out_shape=jax.ShapeDtypeStruct((B, S, W), _bf16),
        grid=(B, S // TS),
        in_specs=[
            tile, tile,
            _layer_spec(l, (CONV_W, W)), _layer_spec(l, (1, W)),
            _layer_spec(l, (W, W)), _layer_spec(l, (1, W)),
            _layer_spec(l, (W, W)), _layer_spec(l, (1, W)),
            _layer_spec(l, (1, W)),
        ],
        out_specs=tile,
        scratch_shapes=[
            pltpu.VMEM((LRU_SLABS, 8 * SEG_PITCH, 128), _f32),
            pltpu.VMEM((SEG, 8, W), _f32),
            pltpu.VMEM((SEG, 8, W), _f32),
            pltpu.VMEM((CONV_W - 1, 8, W), _f32),
            pltpu.VMEM((8, W), _f32),
        ],
        compiler_params=pltpu.CompilerParams(
            dimension_semantics=("arbitrary", "arbitrary"), vmem_limit_bytes=VMEM_LIMIT),
        name="rglru",
    )(xb, gg, p["conv_w"], p["conv_b"], p["wr"], p["br"], p["wi"], p["bi"], p["lam"])


def _merge_kernel(x_ref, mod_ref, ya_ref, yb_ref, yc_ref, zg_ref, bw_ref, mw_ref, lng_ref,
                  lnb_ref, o_ref):
    merged = None
    for n, y_ref in enumerate((ya_ref, yb_ref, yc_ref)):
        gate = _sigmoid(zg_ref[0, :, n * D_MODEL:(n + 1) * D_MODEL].astype(_f32))
        t = gate * _dot(y_ref[0], bw_ref[n])
        merged = t if merged is None else merged + t
    mix = _dot(merged.astype(_bf16), mw_ref[...])
    r = ALPHA * x_ref[0] + mod_ref[0, 2:3, :] * mix
    o_ref[0] = _ln(r) * lng_ref[...] + lnb_ref[...]


def _merge(l, x, mod, ya, yb, yc, zg, p):
    B, S, D = x.shape
    tile = lambda w: pl.BlockSpec((1, TM, w), lambda b, i: (b, i, 0))
    return pl.pallas_call(
        _merge_kernel,
        out_shape=jax.ShapeDtypeStruct((B, S, D), _f32),
        grid=(B, S // TM),
        in_specs=[
            tile(D),
            _mod_spec(l),
            tile(GM_WIDTH), tile(GM_WIDTH), tile(GM_WIDTH), tile(N_BRANCH * D),
            _layer_spec(l, (N_BRANCH, GM_WIDTH, D)),
            _layer_spec(l, (D, D)),
            _layer_spec(l, (1, D)), _layer_spec(l, (1, D)),
        ],
        out_specs=tile(D),
        compiler_params=pltpu.CompilerParams(
            dimension_semantics=("arbitrary", "arbitrary"), vmem_limit_bytes=VMEM_LIMIT),
        name="merge",
    )(x, mod, ya, yb, yc, zg, p["branch_w"], p["mix_w"], p["ln_g0"], p["ln_b0"])


FF_CHUNK = 1024


FF_SUB = 256


def _ffn_kernel(x_ref, mod_ref, w1_ref, b1_ref, w2_ref, b2_ref, lng_ref, lnb_ref, o_ref):
    nsub = TM // FF_SUB
    nchunk = D_FF // FF_CHUNK
    hs = [None] * nsub
    fs = [None] * nsub

    def rows(j):
        return slice(j * FF_SUB, (j + 1) * FF_SUB)

    def pre(j):
        hs[j] = (_ln(x_ref[0, rows(j), :]) * mod_ref[0, 4:5, :] + mod_ref[0, 3:4, :]).astype(_bf16)

    def mm(j, c):
        cs = slice(c * FF_CHUNK, (c + 1) * FF_CHUNK)
        a = jnp.maximum(_dot(hs[j], w1_ref[:, cs]) + b1_ref[:, cs], 0.0)
        t = _dot((a * a).astype(_bf16), w2_ref[cs, :])
        fs[j] = t if c == 0 else fs[j] + t

    def post(j):
        r = ALPHA * x_ref[0, rows(j), :] + mod_ref[0, 5:6, :] * (fs[j] + b2_ref[...])
        o_ref[0, rows(j), :] = _ln(r) * lng_ref[...] + lnb_ref[...]

    pre(0)
    for j in range(nsub):
        for c in range(nchunk):
            mm(j, c)
            if c == 0 and j + 1 < nsub:
                pre(j + 1)
            if c == 1 and j > 0:
                post(j - 1)
    post(nsub - 1)


def _ffn(l, x, mod, p):
    B, S, D = x.shape
    tile = pl.BlockSpec((1, TM, D), lambda b, i: (b, i, 0))
    return pl.pallas_call(
        _ffn_kernel,
        out_shape=jax.ShapeDtypeStruct((B, S, D), _f32),
        grid=(B, S // TM),
        in_specs=[
            tile,
            _mod_spec(l),
            _layer_spec(l, (D, D_FF)), _layer_spec(l, (1, D_FF)),
            _layer_spec(l, (D_FF, D)), _layer_spec(l, (1, D)),
            _layer_spec(l, (1, D)), _layer_spec(l, (1, D)),
        ],
        out_specs=tile,
        compiler_params=pltpu.CompilerParams(
            dimension_semantics=("arbitrary", "arbitrary"), vmem_limit_bytes=VMEM_LIMIT),
        name="ffn",
    )(x, mod, p["w1"], p["b1"], p["w2"], p["b2"], p["ln_g1"], p["ln_b1"])


def _swap_halves(w):
    half = w.shape[-1] // 2
    return jnp.concatenate([w[..., half:], w[..., :half]], axis=-1)


def _head_pad(nope, rope):
    parts = [nope, rope, jnp.zeros(nope.shape[:-1] + (HEAD_PAD - MLA_NOPE - MLA_ROPE,), nope.dtype)]
    out = jnp.concatenate(parts, axis=-1)
    return out.reshape(out.shape[:-2] + (MLA_HEADS * HEAD_PAD,))


def _block_diag(w):
    n, c, d = w.shape[-3:]
    eye = jnp.eye(n, dtype=w.dtype)
    dense = w[..., :, :, None, :] * eye[:, None, :, None]
    return dense.reshape(w.shape[:-3] + (n * c, n * d))


def _prepare_params(in_w, in_b, gm_ln_g, gm_ln_b, gm_ws, gm_bs, mla_qnorm_g, mla_wuq,
                    mla_kvnorm_g, mla_wukv, lru_conv_w, lru_conv_b, lru_wr, lru_br, lru_wi,
                    lru_bi, lru_lambda, branch_w, mix_out_w, ffn_w1, ffn_b1, ffn_w2, ffn_b2,
                    ln_g, ln_b):
    L = in_w.shape[0]
    in_b = in_b[:, None, :]

    def lat_cols(t):
        kr = t[..., OFF_KROPE:OFF_LRU_X]
        z32 = jnp.zeros(kr.shape[:-1] + (MLA_ROPE,), t.dtype)
        return jnp.concatenate([t[..., OFF_QLAT:OFF_KROPE], _swap_halves(kr), z32, kr, z32], axis=-1)

    in_w = in_w.astype(_bf16)
    wuq = mla_wuq.reshape(L, MLA_Q_RANK, MLA_HEADS, MLA_NOPE + MLA_ROPE)
    wuq = jnp.concatenate([wuq, _swap_halves(wuq[..., MLA_NOPE:])], axis=-1)
    wukv = mla_wukv.reshape(L, MLA_KV_RANK, MLA_HEADS, MLA_NOPE + MLA_V)
    wuk = _head_pad(wukv[..., :MLA_NOPE], jnp.zeros((L, MLA_KV_RANK, MLA_HEADS, MLA_ROPE), _f32))
    wuvt = wukv[..., MLA_NOPE:].reshape(L, MLA_KV_RANK, MLA_HEADS * MLA_V).transpose(0, 2, 1)
    row = lambda v: v.reshape(L, 1, -1)
    return dict(
        w_a=in_w[:, :, :OFF_QLAT], b_a=in_b[:, :, :OFF_QLAT],
        w_lat=lat_cols(in_w), b_lat=lat_cols(in_b),
        w_b=in_w[:, :, OFF_LRU_X:], b_b=in_b[:, :, OFF_LRU_X:],
        gm_g=row(gm_ln_g), gm_b=row(gm_ln_b),
        gm_ws=gm_ws, gm_bst=gm_bs.transpose(0, 2, 1),
        q_g=row(mla_qnorm_g),
        wuqt=wuq.reshape(L, MLA_Q_RANK, QK_W).transpose(0, 2, 1).astype(_bf16),
        kv_g=row(mla_kvnorm_g),
        wuk=wuk.astype(_bf16), wuvt=wuvt.astype(_bf16),
        conv_w=lru_conv_w, conv_b=row(lru_conv_b),
        wr=(0.5 * _block_diag(lru_wr)).astype(_bf16), br=0.5 * row(lru_br),
        wi=(0.5 * _block_diag(lru_wi)).astype(_bf16), bi=0.5 * row(lru_bi),
        lam=row(lru_lambda),
        branch_w=branch_w.astype(_bf16), mix_w=mix_out_w.astype(_bf16),
        w1=ffn_w1.astype(_bf16), b1=row(ffn_b1),
        w2=ffn_w2.astype(_bf16), b2=row(ffn_b2),
        ln_g0=row(ln_g[:, 0]), ln_b0=row(ln_b[:, 0]),
        ln_g1=row(ln_g[:, 1]), ln_b1=row(ln_b[:, 1]),
    )


def _rope_tables(seq):
    pos = jnp.arange(seq, dtype=_f32)
    inv = ROPE_BASE ** (-jnp.arange(0, MLA_ROPE, 2, dtype=_f32) / MLA_ROPE)
    ang = pos[:, None] * inv[None, :]
    cos, sin = jnp.cos(ang), jnp.sin(ang)
    ones = jnp.ones((seq, MLA_NOPE), _f32)
    zeros = jnp.zeros((seq, MLA_NOPE), _f32)
    pad = jnp.zeros((seq, HEAD_PAD - MLA_NOPE - MLA_ROPE), _f32)
    cos_q = jnp.concatenate([ones, cos, cos, pad], axis=1)
    cos_k = jnp.concatenate([zeros, cos, cos, pad], axis=1)
    sin_t = jnp.concatenate([zeros, -sin, sin, pad], axis=1)
    return cos_k, sin_t, cos_q.T, sin_t.T


def kernel(x, c, ada_w, ada_b, in_w, in_b, gm_ln_g, gm_ln_b, gm_ws, gm_bs, mla_qnorm_g, mla_wuq, mla_kvnorm_g, mla_wukv, lru_conv_w, lru_conv_b, lru_wr, lru_br, lru_wi, lru_bi, lru_lambda, branch_w, mix_out_w, ffn_w1, ffn_b1, ffn_w2, ffn_b2, ln_g, ln_b):
    B, S, D = x.shape
    assert (B, S, D) == (c.shape[0], S, D_MODEL) and S % TQ == 0 and S % TM == 0
    rope = _rope_tables(S)
    p = _prepare_params(in_w, in_b, gm_ln_g, gm_ln_b, gm_ws, gm_bs, mla_qnorm_g, mla_wuq,
                        mla_kvnorm_g, mla_wukv, lru_conv_w, lru_conv_b, lru_wr, lru_br, lru_wi,
                        lru_bi, lru_lambda, branch_w, mix_out_w, ffn_w1, ffn_b1, ffn_w2, ffn_b2,
                        ln_g, ln_b)
    c8 = jnp.concatenate([c, jnp.zeros((8 - B, D), c.dtype)], axis=0)
    mod_all = _modulation(c8, ada_w, ada_b.reshape(DEPTH, 1, 6 * D))
    m6 = mod_all[:, :B].reshape(DEPTH, B, 6, D)
    m6 = m6 + jnp.array([0.0, 1.0, 1.0, 0.0, 1.0, 1.0], _f32).reshape(1, 1, 6, 1)
    mod = jnp.concatenate([m6, jnp.zeros((DEPTH, B, 2, D), _f32)], axis=2)
    for l in range(DEPTH):
        ya, q, k, vt, xb, gg, zg = _in_proj(l, x, mod, p, rope)
        yb = _attention(q, k, vt)
        yc = _rglru(l, xb, gg, p)
        x = _merge(l, x, mod, ya, yb, yc, zg, p)
        x = _ffn(l, x, mod, p)
    return x
```

```python
import functools
import math

import jax
import jax.numpy as jnp
import numpy as np
from jax import lax
from jax.experimental import pallas as pl
from jax.experimental.pallas import tpu as pltpu

D_MODEL = 1024
DEPTH = 2
CHUNK = 64
GM_BLOCK = 128
GM_WIDTH = 512
GM_GROUPS = 4
MLA_HEADS = 8
MLA_Q_RANK = 256
MLA_KV_RANK = 128
MLA_NOPE = 64
MLA_ROPE = 32
MLA_V = 64
ROPE_BASE = 10000.0
LRU_WIDTH = 512
LRU_BLOCKS = 8
CONV_W = 4
LRU_C = 8.0
N_BRANCH = 3
D_FF = 4 * D_MODEL
ALPHA = (2.0 * DEPTH) ** 0.25
LN_EPS = 1e-5
RMS_EPS = 1e-6

OFF_QLAT = 2 * GM_WIDTH
OFF_KVLAT = OFF_QLAT + MLA_Q_RANK
OFF_KROPE = OFF_KVLAT + MLA_KV_RANK
OFF_LRU_X = OFF_KROPE + MLA_ROPE
OFF_LRU_G = OFF_LRU_X + LRU_WIDTH
OFF_GATE = OFF_LRU_G + LRU_WIDTH
N_IN = OFF_GATE + N_BRANCH * D_MODEL

HEAD_PAD = 128
QK_W = MLA_HEADS * HEAD_PAD
OFF_GM = 0
LAT_W = MLA_Q_RANK + MLA_KV_RANK + HEAD_PAD

TM = 512
TQ = 1024
TK = 1024
TS = 512
VMEM_LIMIT = 56 * 1024 * 1024
NEG_BIG = -1e30
LOG2E = math.log2(math.e)

_bf16 = jnp.bfloat16
_f32 = jnp.float32


def _ln(x):
    mu = jnp.mean(x, axis=-1, keepdims=True)
    xc = x - mu
    var = jnp.mean(xc * xc, axis=-1, keepdims=True)
    return xc * lax.rsqrt(var + LN_EPS)


def _rms(x, g):
    return x * lax.rsqrt(jnp.mean(x * x, axis=-1, keepdims=True) + RMS_EPS) * g


def _sigmoid(x):
    return 0.5 * jnp.tanh(0.5 * x) + 0.5


def _dot(a, b):
    return jnp.dot(a, b, preferred_element_type=_f32)


def _dot_nt(a, b):
    return lax.dot_general(a, b, (((1,), (1,)), ((), ())), preferred_element_type=_f32)


def _mod_spec(l):
    return pl.BlockSpec((None, 1, 8, D_MODEL), lambda b, i: (l, b, 0, 0))


def _layer_spec(l, shape):
    nd = len(shape)
    return pl.BlockSpec((None,) + tuple(shape), lambda *_: (l,) + (0,) * nd,
                        pipeline_mode=pl.Buffered(1))


def _mod_kernel(c_ref, w_ref, b_ref, o_ref):
    c = c_ref[...]
    c_act = (c * jax.nn.sigmoid(c)).astype(_bf16)
    o_ref[0] = _dot(c_act, w_ref[0].astype(_bf16)) + b_ref[0]


def _modulation(c8, ada_w, ada_b):
    L = ada_w.shape[0]
    nb = 1536
    return pl.pallas_call(
        _mod_kernel,
        out_shape=jax.ShapeDtypeStruct((L, 8, 6 * D_MODEL), _f32),
        grid=(L, 6 * D_MODEL // nb),
        in_specs=[
            pl.BlockSpec((8, D_MODEL), lambda l, j: (0, 0)),
            pl.BlockSpec((1, D_MODEL, nb), lambda l, j: (l, 0, j)),
            pl.BlockSpec((1, 1, nb), lambda l, j: (l, 0, j)),
        ],
        out_specs=pl.BlockSpec((1, 8, nb), lambda l, j: (l, 0, j)),
        compiler_params=pltpu.CompilerParams(
            dimension_semantics=("arbitrary", "arbitrary"), vmem_limit_bytes=VMEM_LIMIT),
        name="adaln_mod",
    )(c8, ada_w, ada_b)


def _in_proj_kernel(x_ref, mod_ref, wa_ref, ba_ref, wlat_ref, blat_ref, wb_ref, bb_ref,
                    gmg_ref, gmb_ref, ws_ref, bst_ref,
                    qg_ref, wuqt_ref, kvg_ref, wuk_ref, wuvt_ref, cosk_ref, sin_ref, cosqt_ref, sint_ref,
                    ya_ref, q_ref, k_ref, vt_ref, xb_ref, gg_ref, zg_ref):
    x = x_ref[0]
    h = (_ln(x) * mod_ref[0, 1:2, :] + mod_ref[0, 0:1, :]).astype(_bf16)

    def proj(w_ref, b_ref, lo, hi):
        return _dot(h, w_ref[:, lo:hi]) + b_ref[:, lo:hi]

    zg = jax.nn.gelu(proj(wa_ref, ba_ref, OFF_GM, OFF_QLAT))
    u = zg[:, :GM_WIDTH]
    v = (_ln(zg[:, GM_WIDTH:]) * gmg_ref[...] + gmb_ref[...]).astype(_bf16)
    row = lax.broadcasted_iota(jnp.int32, (GM_BLOCK, GM_BLOCK), 0) // CHUNK
    col = lax.broadcasted_iota(jnp.int32, (GM_BLOCK, GM_BLOCK), 1) // CHUNK
    gch = GM_WIDTH // GM_GROUPS
    for g in range(GM_GROUPS):
        wg = jnp.where(row >= col, ws_ref[g], 0.0).astype(_bf16)
        bias = bst_ref[:, g:g + 1]
        for r in range(TM // GM_BLOCK):
            rs = slice(r * GM_BLOCK, (r + 1) * GM_BLOCK)
            cs = slice(g * gch, (g + 1) * gch)
            f = _dot(wg, v[rs, cs]) + bias
            ya_ref[0, rs, cs] = (u[rs, cs] * f).astype(_bf16)

    zlat = proj(wlat_ref, blat_ref, 0, LAT_W)
    qn = _rms(zlat[:, :MLA_Q_RANK], qg_ref[...]).astype(_bf16)
    qscale = (MLA_NOPE + MLA_ROPE) ** -0.5 * LOG2E
    qmt = _dot_nt(wuqt_ref[...], qn)
    cos_t = cosqt_ref[...]
    sin_t = sint_ref[...]
    for hd in range(MLA_HEADS):
        hs = slice(hd * HEAD_PAD, (hd + 1) * HEAD_PAD)
        g = qmt[hs, :]
        rot = pltpu.roll(g, HEAD_PAD - MLA_ROPE, 0)
        q_ref[0, hs, :] = ((g * cos_t + rot * sin_t) * qscale).astype(_bf16)

    ckv = _rms(zlat[:, MLA_Q_RANK:MLA_Q_RANK + MLA_KV_RANK], kvg_ref[...]).astype(_bf16)
    gk = zlat[:, MLA_Q_RANK + MLA_KV_RANK:]
    kr = gk * cosk_ref[...] + pltpu.roll(gk, MLA_NOPE, 1) * sin_ref[...]
    kn = _dot(ckv, wuk_ref[...])
    for hd in range(MLA_HEADS):
        hs = slice(hd * HEAD_PAD, (hd + 1) * HEAD_PAD)
        k_ref[0, :, hs] = (kn[:, hs] + kr).astype(_bf16)
    vt_ref[0] = _dot_nt(wuvt_ref[...], ckv).astype(_bf16)

    xb_ref[0] = proj(wb_ref, bb_ref, 0, LRU_WIDTH)
    gg_ref[0] = jax.nn.gelu(proj(wb_ref, bb_ref, LRU_WIDTH, 2 * LRU_WIDTH)).astype(_bf16)
    for n in range(N_BRANCH):
        lo = 2 * LRU_WIDTH + n * D_MODEL
        zg_ref[0, :, n * D_MODEL:(n + 1) * D_MODEL] = proj(wb_ref, bb_ref, lo, lo + D_MODEL).astype(_bf16)


def _in_proj(l, x, mod, p, rope):
    B, S, D = x.shape
    tile = lambda w: pl.BlockSpec((1, TM, w), lambda b, i: (b, i, 0))
    outs = [
        jax.ShapeDtypeStruct((B, S, GM_WIDTH), _bf16),
        jax.ShapeDtypeStruct((B, QK_W, S), _bf16),
        jax.ShapeDtypeStruct((B, S, QK_W), _bf16),
        jax.ShapeDtypeStruct((B, MLA_HEADS * MLA_V, S), _bf16),
        jax.ShapeDtypeStruct((B, S, LRU_WIDTH), _f32),
        jax.ShapeDtypeStruct((B, S, LRU_WIDTH), _bf16),
        jax.ShapeDtypeStruct((B, S, N_BRANCH * D_MODEL), _bf16),
    ]
    return pl.pallas_call(
        _in_proj_kernel,
        out_shape=outs,
        grid=(B, S // TM),
        in_specs=[
            tile(D),
            _mod_spec(l),
            _layer_spec(l, (D, OFF_QLAT)), _layer_spec(l, (1, OFF_QLAT)),
            _layer_spec(l, (D, LAT_W)), _layer_spec(l, (1, LAT_W)),
            _layer_spec(l, (D, N_IN - OFF_LRU_X)), _layer_spec(l, (1, N_IN - OFF_LRU_X)),
            _layer_spec(l, (1, GM_WIDTH)),
            _layer_spec(l, (1, GM_WIDTH)),
            _layer_spec(l, (GM_GROUPS, GM_BLOCK, GM_BLOCK)),
            _layer_spec(l, (GM_BLOCK, GM_GROUPS)),
            _layer_spec(l, (1, MLA_Q_RANK)),
            _layer_spec(l, (QK_W, MLA_Q_RANK)),
            _layer_spec(l, (1, MLA_KV_RANK)),
            _layer_spec(l, (MLA_KV_RANK, QK_W)),
            _layer_spec(l, (MLA_HEADS * MLA_V, MLA_KV_RANK)),
            pl.BlockSpec((TM, HEAD_PAD), lambda b, i: (i, 0)),
            pl.BlockSpec((TM, HEAD_PAD), lambda b, i: (i, 0)),
            pl.BlockSpec((HEAD_PAD, TM), lambda b, i: (0, i)),
            pl.BlockSpec((HEAD_PAD, TM), lambda b, i: (0, i)),
        ],
        out_specs=[
            tile(GM_WIDTH),
            pl.BlockSpec((1, QK_W, TM), lambda b, i: (b, 0, i)),
            tile(QK_W),
            pl.BlockSpec((1, MLA_HEADS * MLA_V, TM), lambda b, i: (b, 0, i)),
            tile(LRU_WIDTH), tile(LRU_WIDTH), tile(N_BRANCH * D_MODEL),
        ],
        compiler_params=pltpu.CompilerParams(
            dimension_semantics=("arbitrary", "arbitrary"), vmem_limit_bytes=VMEM_LIMIT),
        name="in_proj",
    )(x, mod, p["w_a"], p["b_a"], p["w_lat"], p["b_lat"], p["w_b"], p["b_b"],
      p["gm_g"], p["gm_b"], p["gm_ws"], p["gm_bst"],
      p["q_g"], p["wuqt"], p["kv_g"], p["wuk"], p["wuvt"], *rope)


SUB = 256
ACC_ROWS = MLA_V + 16
LAG_STATS, LAG_PV, LAG_ACC = 3, 5, 8


def _attn_units(rel):
    units = []
    for kh in range(TK // SUB):
        for hd in range(MLA_HEADS):
            for c in range(TQ // SUB):
                mode = "full"
                if rel is not None:
                    koff, qoff = rel + kh * SUB, c * SUB
                    if koff >= qoff + SUB:
                        continue
                    if koff + SUB > qoff:
                        assert koff == qoff
                        mode = "diag"
                units.append((kh, hd, c, mode))
    return units


def _attn_kernel(qi_ref, ki_ref, q_ref, k_ref, vt_ref, o_ref, m_sc, acc_sc, o_sc):
    step = pl.program_id(1)
    qi = qi_ref[step]
    ki = ki_ref[step]
    nsub = SUB // 8

    @pl.when(ki == 0)
    def _():
        m_sc[...] = jnp.full(m_sc.shape, NEG_BIG, _f32)
        acc_sc[...] = jnp.zeros(acc_sc.shape, _f32)

    def stage_qk(unit, st, visible):
        kh, hd, c, mode = unit
        hs = slice(hd * HEAD_PAD, (hd + 1) * HEAD_PAD)
        s = _dot(k_ref[0, kh * SUB:(kh + 1) * SUB, hs], q_ref[0, hs, c * SUB:(c + 1) * SUB])
        if mode == "diag":
            s = jnp.where(visible, s, NEG_BIG)
        st["s3"] = s.reshape(nsub, 8, SUB)

    def stage_stats(unit, st, visible):
        kh, hd, c, mode = unit
        cm = jnp.max(st["s3"], axis=0)
        for d in (4, 2, 1):
            cm = jnp.maximum(cm, pltpu.roll(cm, d, 0))
        msl = (slice(hd * 8, (hd + 1) * 8), slice(c * SUB, (c + 1) * SUB))
        m_old = m_sc[msl]
        m_new = jnp.maximum(m_old, cm)
        m_sc[msl] = m_new
        st["alpha"] = jnp.exp2(m_old - m_new)
        st["p"] = jnp.exp2(st.pop("s3") - m_new[None]).reshape(SUB, SUB).astype(_bf16)

    def stage_pv(unit, st, visible):
        kh, hd, c, _ = unit
        v_t = vt_ref[0, hd * MLA_V:(hd + 1) * MLA_V, kh * SUB:(kh + 1) * SUB]
        v_ext = jnp.concatenate([v_t, jnp.ones((ACC_ROWS - MLA_V, SUB), _bf16)], axis=0)
        st["pv"] = _dot(v_ext, st.pop("p")).reshape(ACC_ROWS // 8, 8, SUB)

    def stage_acc(unit, st, visible):
        kh, hd, c, _ = unit
        cs = slice(c * SUB, (c + 1) * SUB)
        acc = acc_sc[hd, :, cs].reshape(ACC_ROWS // 8, 8, SUB)
        acc_sc[hd, :, cs] = (st.pop("alpha")[None] * acc + st.pop("pv")).reshape(ACC_ROWS, SUB)

    stages = ((stage_qk, 0), (stage_stats, LAG_STATS), (stage_pv, LAG_PV), (stage_acc, LAG_ACC))

    def update(rel):
        units = _attn_units(rel)
        visible = None
        if rel is not None:
            row = lax.broadcasted_iota(jnp.int32, (SUB, SUB), 0) // CHUNK
            col = lax.broadcasted_iota(jnp.int32, (SUB, SUB), 1) // CHUNK
            visible = row <= col
        n = len(units)
        state = [{} for _ in units]
        for t in range(n + LAG_ACC):
            for fn, lag in stages:
                if 0 <= t - lag < n:
                    fn(units[t - lag], state[t - lag], visible)

    rel = ki * TK - qi * TQ
    for r in range(0, TQ, TK):
        pl.when(rel == r)(functools.partial(update, r))
    pl.when(rel < 0)(functools.partial(update, None))

    @pl.when(rel == TQ - TK)
    def _():
        for hd in range(MLA_HEADS):
            inv = 1.0 / acc_sc[hd, MLA_V:MLA_V + 8, :]
            o = acc_sc[hd, 0:MLA_V, :].reshape(MLA_V // 8, 8, TQ) * inv[None]
            o_sc[hd * MLA_V:(hd + 1) * MLA_V, :] = o.reshape(MLA_V, TQ)
        o_ref[0] = o_sc[...].T.astype(_bf16)


def _attention(qt, k, vt):
    B, _, S = qt.shape
    nq = S // TQ
    ratio = TQ // TK
    qi_tab, ki_tab = [], []
    for i in range(nq):
        for j in range((i + 1) * ratio):
            qi_tab.append(i)
            ki_tab.append(j)
    n_steps = len(qi_tab)
    qi_tab = jnp.asarray(np.array(qi_tab, np.int32))
    ki_tab = jnp.asarray(np.array(ki_tab, np.int32))
    grid_spec = pltpu.PrefetchScalarGridSpec(
        num_scalar_prefetch=2,
        grid=(B, n_steps),
        in_specs=[
            pl.BlockSpec((1, QK_W, TQ), lambda b, s, qi, ki: (b, 0, qi[s])),
            pl.BlockSpec((1, TK, QK_W), lambda b, s, qi, ki: (b, ki[s], 0)),
            pl.BlockSpec((1, MLA_HEADS * MLA_V, TK), lambda b, s, qi, ki: (b, 0, ki[s])),
        ],
        out_specs=pl.BlockSpec((1, TQ, MLA_HEADS * MLA_V), lambda b, s, qi, ki: (b, qi[s], 0)),
        scratch_shapes=[
            pltpu.VMEM((MLA_HEADS * 8, TQ), _f32),
            pltpu.VMEM((MLA_HEADS, ACC_ROWS, TQ), _f32),
            pltpu.VMEM((MLA_HEADS * MLA_V, TQ), _f32),
        ],
    )
    return pl.pallas_call(
        _attn_kernel,
        out_shape=jax.ShapeDtypeStruct((B, S, MLA_HEADS * MLA_V), _bf16),
        grid_spec=grid_spec,
        compiler_params=pltpu.CompilerParams(
            dimension_semantics=("arbitrary", "arbitrary"), vmem_limit_bytes=VMEM_LIMIT),
        name="mla_attn",
    )(qi_tab, ki_tab, qt, k, vt)


SEG = TS // 8
SEG_PITCH = SEG + 8
LRU_SLABS = LRU_WIDTH // 128


def _rglru_kernel(xb_ref, gg_ref, cw_ref, cb_ref, wr_ref, br_ref, wi_ref, bi_ref, lam_ref,
                  y_ref, seg_sc, hl_sc, cp_sc, tail_sc, carry_sc):
    i = pl.program_id(1)

    @pl.when(i == 0)
    def _():
        tail_sc[...] = jnp.zeros(tail_sc.shape, _f32)
        carry_sc[...] = jnp.zeros(carry_sc.shape, _f32)

    for sl in range(LRU_SLABS):
        for r in range(8):
            seg_sc[sl, r * SEG_PITCH:r * SEG_PITCH + SEG, :] = (
                xb_ref[0, r * SEG:(r + 1) * SEG, sl * 128:(sl + 1) * 128])

    rows = lax.broadcasted_iota(jnp.int32, (8, 128), 0)
    xc_slabs = []
    for sl in range(LRU_SLABS):
        ls = slice(sl * 128, (sl + 1) * 128)
        xg = [seg_sc.at[sl][pl.ds(g, 8, stride=SEG_PITCH), :] for g in range(SEG)]
        before = [jnp.where(rows == 0, tail_sc[j - 1, :, ls], pltpu.roll(xg[SEG - j], 1, 0))
                  for j in range(1, CONV_W)]
        xs = before[::-1] + xg
        xc_g = []
        for g in range(SEG):
            acc = cb_ref[:, ls] + xs[g + CONV_W - 1] * cw_ref[CONV_W - 1:CONV_W, ls]
            for j in range(1, CONV_W):
                acc = acc + xs[g + CONV_W - 1 - j] * cw_ref[CONV_W - 1 - j:CONV_W - j, ls]
            xc_g.append(acc)
        for j in range(1, CONV_W):
            tail_sc[j - 1, :, ls] = jnp.broadcast_to(xg[SEG - j][7:8, :], (8, 128))
        xc_slabs.append(jnp.concatenate(xc_g, axis=0))
    xc = jnp.concatenate(xc_slabs, axis=1)

    xcb = xc.astype(_bf16)
    th_r = jnp.tanh(_dot(xcb, wr_ref[...]) + br_ref[...])
    th_i = jnp.tanh(_dot(xcb, wi_ref[...]) + bi_ref[...])
    lam = lam_ref[...]
    softplus = jnp.maximum(-lam, 0.0) + jnp.log1p(jnp.exp(-jnp.abs(lam)))
    half = (0.5 * LRU_C) * softplus
    nla = th_r * half + half
    a = jnp.exp(-nla)
    z = jnp.tanh(nla) * (a * a + 1.0)
    gain = jnp.where(z > 0.0, z * lax.rsqrt(z), 0.0)
    b = gain * ((0.5 * th_i + 0.5) * xc)

    h = [None] * LRU_SLABS
    cp = [None] * LRU_SLABS
    for g in range(SEG):
        for sl in range(LRU_SLABS):
            ls = slice(sl * 128, (sl + 1) * 128)
            a_g = a[g * 8:(g + 1) * 8, ls]
            b_g = b[g * 8:(g + 1) * 8, ls]
            h[sl] = b_g if g == 0 else a_g * h[sl] + b_g
            cp[sl] = a_g if g == 0 else a_g * cp[sl]
            hl_sc[g, :, ls] = h[sl]
            cp_sc[g, :, ls] = cp[sl]

    for sl in range(LRU_SLABS):
        ls = slice(sl * 128, (sl + 1) * 128)
        a8 = jnp.where(rows == 0, 0.0, pltpu.roll(cp[sl], 1, 0))
        b8 = jnp.where(rows == 0, carry_sc[:, ls], pltpu.roll(h[sl], 1, 0))
        for d in (1, 2, 4):
            ok = rows >= d
            a_sh = pltpu.roll(a8, d, 0)
            b_sh = pltpu.roll(b8, d, 0)
            b8 = jnp.where(ok, a8 * b_sh + b8, b8)
            a8 = jnp.where(ok, a8 * a_sh, a8)
        inflow = b8
        for g in range(SEG):
            hf = hl_sc[g, :, ls] + cp_sc[g, :, ls] * inflow
            seg_sc.at[sl][pl.ds(g, 8, stride=SEG_PITCH), :] = hf
        carry_sc[:, ls] = jnp.broadcast_to(hf[7:8, :], (8, 128))
        for r in range(8):
            rs = slice(r * SEG, (r + 1) * SEG)
            hr = seg_sc[sl, r * SEG_PITCH:r * SEG_PITCH + SEG, :]
            y_ref[0, rs, ls] = (hr * gg_ref[0, rs, ls].astype(_f32)).astype(_bf16)


def _rglru(l, xb, gg, p):
    B, S, W = xb.shape
    tile = pl.BlockSpec((1, TS, W), lambda b, i: (b, i, 0))
    return pl.pallas_call(
        _rglru_kernel,
        out_shape=jax.ShapeDtypeStruct((B, S, W), _bf16),
        grid=(B, S // TS),
        in_specs=[
            tile, tile,
            _layer_spec(l, (CONV_W, W)), _layer_spec(l, (1, W)),
            _layer_spec(l, (W, W)), _layer_spec(l, (1, W)),
            _layer_spec(l, (W, W)), _layer_spec(l, (1, W)),
            _layer_spec(l, (1, W)),
        ],
        out_specs=tile,
        scratch_shapes=[
            pltpu.VMEM((LRU_SLABS, 8 * SEG_PITCH, 128), _f32),
            pltpu.VMEM((SEG, 8, W), _f32),
            pltpu.VMEM((SEG, 8, W), _f32),
            pltpu.VMEM((CONV_W - 1, 8, W), _f32),
            pltpu.VMEM((8, W), _f32),
        ],
        compiler_params=pltpu.CompilerParams(
            dimension_semantics=("arbitrary", "arbitrary"), vmem_limit_bytes=VMEM_LIMIT),
        name="rglru",
    )(xb, gg, p["conv_w"], p["conv_b"], p["wr"], p["br"], p["wi"], p["bi"], p["lam"])


def _merge_kernel(x_ref, mod_ref, ya_ref, yb_ref, yc_ref, zg_ref, bw_ref, mw_ref, lng_ref,
                  lnb_ref, o_ref):
    merged = None
    for n, y_ref in enumerate((ya_ref, yb_ref, yc_ref)):
        gate = _sigmoid(zg_ref[0, :, n * D_MODEL:(n + 1) * D_MODEL].astype(_f32))
        t = gate * _dot(y_ref[0], bw_ref[n])
        merged = t if merged is None else merged + t
    mix = _dot(merged.astype(_bf16), mw_ref[...])
    r = ALPHA * x_ref[0] + mod_ref[0, 2:3, :] * mix
    o_ref[0] = _ln(r) * lng_ref[...] + lnb_ref[...]


def _merge(l, x, mod, ya, yb, yc, zg, p):
    B, S, D = x.shape
    tile = lambda w: pl.BlockSpec((1, TM, w), lambda b, i: (b, i, 0))
    return pl.pallas_call(
        _merge_kernel,
        out_shape=jax.ShapeDtypeStruct((B, S, D), _f32),
        grid=(B, S // TM),
        in_specs=[
            tile(D),
            _mod_spec(l),
            tile(GM_WIDTH), tile(GM_WIDTH), tile(GM_WIDTH), tile(N_BRANCH * D),
            _layer_spec(l, (N_BRANCH, GM_WIDTH, D)),
            _layer_spec(l, (D, D)),
            _layer_spec(l, (1, D)), _layer_spec(l, (1, D)),
        ],
        out_specs=tile(D),
        compiler_params=pltpu.CompilerParams(
            dimension_semantics=("arbitrary", "arbitrary"), vmem_limit_bytes=VMEM_LIMIT),
        name="merge",
    )(x, mod, ya, yb, yc, zg, p["branch_w"], p["mix_w"], p["ln_g0"], p["ln_b0"])


FF_CHUNK = 1024


FF_SUB = 256


def _ffn_kernel(x_ref, mod_ref, w1_ref, b1_ref, w2_ref, b2_ref, lng_ref, lnb_ref, o_ref):
    nsub = TM // FF_SUB
    nchunk = D_FF // FF_CHUNK
    hs = [None] * nsub
    fs = [None] * nsub

    def rows(j):
        return slice(j * FF_SUB, (j + 1) * FF_SUB)

    def pre(j):
        hs[j] = (_ln(x_ref[0, rows(j), :]) * mod_ref[0, 4:5, :] + mod_ref[0, 3:4, :]).astype(_bf16)

    def mm(j, c):
        cs = slice(c * FF_CHUNK, (c + 1) * FF_CHUNK)
        a = jnp.maximum(_dot(hs[j], w1_ref[:, cs]) + b1_ref[:, cs], 0.0)
        t = _dot((a * a).astype(_bf16), w2_ref[cs, :])
        fs[j] = t if c == 0 else fs[j] + t

    def post(j):
        r = ALPHA * x_ref[0, rows(j), :] + mod_ref[0, 5:6, :] * (fs[j] + b2_ref[...])
        o_ref[0, rows(j), :] = _ln(r) * lng_ref[...] + lnb_ref[...]

    pre(0)
    for j in range(nsub):
        for c in range(nchunk):
            mm(j, c)
            if c == 0 and j + 1 < nsub:
                pre(j + 1)
            if c == 1 and j > 0:
                post(j - 1)
    post(nsub - 1)


def _ffn(l, x, mod, p):
    B, S, D = x.shape
    tile = pl.BlockSpec((1, TM, D), lambda b, i: (b, i, 0))
    return pl.pallas_call(
        _ffn_kernel,
        out_shape=jax.ShapeDtypeStruct((B, S, D), _f32),
        grid=(B, S // TM),
        in_specs=[
            tile,
            _mod_spec(l),
            _layer_spec(l, (D, D_FF)), _layer_spec(l, (1, D_FF)),
            _layer_spec(l, (D_FF, D)), _layer_spec(l, (1, D)),
            _layer_spec(l, (1, D)), _layer_spec(l, (1, D)),
        ],
        out_specs=tile,
        compiler_params=pltpu.CompilerParams(
            dimension_semantics=("arbitrary", "arbitrary"), vmem_limit_bytes=VMEM_LIMIT),
        name="ffn",
    )(x, mod, p["w1"], p["b1"], p["w2"], p["b2"], p["ln_g1"], p["ln_b1"])


def _swap_halves(w):
    half = w.shape[-1] // 2
    return jnp.concatenate([w[..., half:], w[..., :half]], axis=-1)


def _head_pad(nope, rope):
    parts = [nope, rope, jnp.zeros(nope.shape[:-1] + (HEAD_PAD - MLA_NOPE - MLA_ROPE,), nope.dtype)]
    out = jnp.concatenate(parts, axis=-1)
    return out.reshape(out.shape[:-2] + (MLA_HEADS * HEAD_PAD,))


def _block_diag(w):
    n, c, d = w.shape[-3:]
    eye = jnp.eye(n, dtype=w.dtype)
    dense = w[..., :, :, None, :] * eye[:, None, :, None]
    return dense.reshape(w.shape[:-3] + (n * c, n * d))


def _prepare_params(in_w, in_b, gm_ln_g, gm_ln_b, gm_ws, gm_bs, mla_qnorm_g, mla_wuq,
                    mla_kvnorm_g, mla_wukv, lru_conv_w, lru_conv_b, lru_wr, lru_br, lru_wi,
                    lru_bi, lru_lambda, branch_w, mix_out_w, ffn_w1, ffn_b1, ffn_w2, ffn_b2,
                    ln_g, ln_b):
    L = in_w.shape[0]
    in_b = in_b[:, None, :]

    def lat_cols(t):
        kr = t[..., OFF_KROPE:OFF_LRU_X]
        z32 = jnp.zeros(kr.shape[:-1] + (MLA_ROPE,), t.dtype)
        return jnp.concatenate([t[..., OFF_QLAT:OFF_KROPE], _swap_halves(kr), z32, kr, z32], axis=-1)

    in_w = in_w.astype(_bf16)
    wuq = mla_wuq.reshape(L, MLA_Q_RANK, MLA_HEADS, MLA_NOPE + MLA_ROPE)
    wuq = jnp.concatenate([wuq, _swap_halves(wuq[..., MLA_NOPE:])], axis=-1)
    wukv = mla_wukv.reshape(L, MLA_KV_RANK, MLA_HEADS, MLA_NOPE + MLA_V)
    wuk = _head_pad(wukv[..., :MLA_NOPE], jnp.zeros((L, MLA_KV_RANK, MLA_HEADS, MLA_ROPE), _f32))
    wuvt = wukv[..., MLA_NOPE:].reshape(L, MLA_KV_RANK, MLA_HEADS * MLA_V).transpose(0, 2, 1)
    row = lambda v: v.reshape(L, 1, -1)
    return dict(
        w_a=in_w[:, :, :OFF_QLAT], b_a=in_b[:, :, :OFF_QLAT],
        w_lat=lat_cols(in_w), b_lat=lat_cols(in_b),
        w_b=in_w[:, :, OFF_LRU_X:], b_b=in_b[:, :, OFF_LRU_X:],
        gm_g=row(gm_ln_g), gm_b=row(gm_ln_b),
        gm_ws=gm_ws, gm_bst=gm_bs.transpose(0, 2, 1),
        q_g=row(mla_qnorm_g),
        wuqt=wuq.reshape(L, MLA_Q_RANK, QK_W).transpose(0, 2, 1).astype(_bf16),
        kv_g=row(mla_kvnorm_g),
        wuk=wuk.astype(_bf16), wuvt=wuvt.astype(_bf16),
        conv_w=lru_conv_w, conv_b=row(lru_conv_b),
        wr=(0.5 * _block_diag(lru_wr)).astype(_bf16), br=0.5 * row(lru_br),
        wi=(0.5 * _block_diag(lru_wi)).astype(_bf16), bi=0.5 * row(lru_bi),
        lam=row(lru_lambda),
        branch_w=branch_w.astype(_bf16), mix_w=mix_out_w.astype(_bf16),
        w1=ffn_w1.astype(_bf16), b1=row(ffn_b1),
        w2=ffn_w2.astype(_bf16), b2=row(ffn_b2),
        ln_g0=row(ln_g[:, 0]), ln_b0=row(ln_b[:, 0]),
        ln_g1=row(ln_g[:, 1]), ln_b1=row(ln_b[:, 1]),
    )


def _rope_tables(seq):
    pos = jnp.arange(seq, dtype=_f32)
    inv = ROPE_BASE ** (-jnp.arange(0, MLA_ROPE, 2, dtype=_f32) / MLA_ROPE)
    ang = pos[:, None] * inv[None, :]
    cos, sin = jnp.cos(ang), jnp.sin(ang)
    ones = jnp.ones((seq, MLA_NOPE), _f32)
    zeros = jnp.zeros((seq, MLA_NOPE), _f32)
    pad = jnp.zeros((seq, HEAD_PAD - MLA_NOPE - MLA_ROPE), _f32)
    cos_q = jnp.concatenate([ones, cos, cos, pad], axis=1)
    cos_k = jnp.concatenate([zeros, cos, cos, pad], axis=1)
    sin_t = jnp.concatenate([zeros, -sin, sin, pad], axis=1)
    return cos_k, sin_t, cos_q.T, sin_t.T


def kernel(x, c, ada_w, ada_b, in_w, in_b, gm_ln_g, gm_ln_b, gm_ws, gm_bs, mla_qnorm_g, mla_wuq, mla_kvnorm_g, mla_wukv, lru_conv_w, lru_conv_b, lru_wr, lru_br, lru_wi, lru_bi, lru_lambda, branch_w, mix_out_w, ffn_w1, ffn_b1, ffn_w2, ffn_b2, ln_g, ln_b):
    B, S, D = x.shape
    assert (B, S, D) == (c.shape[0], S, D_MODEL) and S % TQ == 0 and S % TM == 0
    rope = _rope_tables(S)
    p = _prepare_params(in_w, in_b, gm_ln_g, gm_ln_b, gm_ws, gm_bs, mla_qnorm_g, mla_wuq,
                        mla_kvnorm_g, mla_wukv, lru_conv_w, lru_conv_b, lru_wr, lru_br, lru_wi,
                        lru_bi, lru_lambda, branch_w, mix_out_w, ffn_w1, ffn_b1, ffn_w2, ffn_b2,
                        ln_g, ln_b)
    c8 = jnp.concatenate([c, jnp.zeros((8 - B, D), c.dtype)], axis=0)
    mod_all = _modulation(c8, ada_w, ada_b.reshape(DEPTH, 1, 6 * D))
    m6 = mod_all[:, :B].reshape(DEPTH, B, 6, D)
    m6 = m6 + jnp.array([0.0, 1.0, 1.0, 0.0, 1.0, 1.0], _f32).reshape(1, 1, 6, 1)
    mod = jnp.concatenate([m6, jnp.zeros((DEPTH, B, 2, D), _f32)], axis=2)
    for l in range(DEPTH):
        ya, q, k, vt, xb, gg, zg = _in_proj(l, x, mod, p, rope)
        yb = _attention(q, k, vt)
        yc = _rglru(l, xb, gg, p)
        x = _merge(l, x, mod, ya, yb, yc, zg, p)
        x = _ffn(l, x, mod, p)
    return x
```

```python
import functools
import math

import jax
import jax.numpy as jnp
import numpy as np
from jax import lax
from jax.experimental import pallas as pl
from jax.experimental.pallas import tpu as pltpu

D_MODEL = 1024
DEPTH = 2
CHUNK = 64
GM_BLOCK = 128
GM_WIDTH = 512
GM_GROUPS = 4
MLA_HEADS = 8
MLA_Q_RANK = 256
MLA_KV_RANK = 128
MLA_NOPE = 64
MLA_ROPE = 32
MLA_V = 64
ROPE_BASE = 10000.0
LRU_WIDTH = 512
LRU_BLOCKS = 8
CONV_W = 4
LRU_C = 8.0
N_BRANCH = 3
D_FF = 4 * D_MODEL
ALPHA = (2.0 * DEPTH) ** 0.25
LN_EPS = 1e-5
RMS_EPS = 1e-6

OFF_QLAT = 2 * GM_WIDTH
OFF_KVLAT = OFF_QLAT + MLA_Q_RANK
OFF_KROPE = OFF_KVLAT + MLA_KV_RANK
OFF_LRU_X = OFF_KROPE + MLA_ROPE
OFF_LRU_G = OFF_LRU_X + LRU_WIDTH
OFF_GATE = OFF_LRU_G + LRU_WIDTH
N_IN = OFF_GATE + N_BRANCH * D_MODEL

HEAD_PAD = 128
QK_W = MLA_HEADS * HEAD_PAD
OFF_GM = 0
LAT_W = MLA_Q_RANK + MLA_KV_RANK + HEAD_PAD

TM = 512
TQ = 1024
TK = 1024
TS = 512
VMEM_LIMIT = 56 * 1024 * 1024
NEG_BIG = -1e30
LOG2E = math.log2(math.e)

_bf16 = jnp.bfloat16
_f32 = jnp.float32


def _ln(x):
    mu = jnp.mean(x, axis=-1, keepdims=True)
    xc = x - mu
    var = jnp.mean(xc * xc, axis=-1, keepdims=True)
    return xc * lax.rsqrt(var + LN_EPS)


def _rms(x, g):
    return x * lax.rsqrt(jnp.mean(x * x, axis=-1, keepdims=True) + RMS_EPS) * g


def _sigmoid(x):
    return 0.5 * jnp.tanh(0.5 * x) + 0.5


def _dot(a, b):
    return jnp.dot(a, b, preferred_element_type=_f32)


def _dot_nt(a, b):
    return lax.dot_general(a, b, (((1,), (1,)), ((), ())), preferred_element_type=_f32)


def _mod_spec(l):
    return pl.BlockSpec((None, 1, 8, D_MODEL), lambda b, i: (l, b, 0, 0))


def _layer_spec(l, shape):
    nd = len(shape)
    return pl.BlockSpec((None,) + tuple(shape), lambda *_: (l,) + (0,) * nd,
                        pipeline_mode=pl.Buffered(1))


def _mod_kernel(c_ref, w_ref, b_ref, o_ref):
    c = c_ref[...]
    c_act = (c * jax.nn.sigmoid(c)).astype(_bf16)
    o_ref[0] = _dot(c_act, w_ref[0].astype(_bf16)) + b_ref[0]


def _modulation(c8, ada_w, ada_b):
    L = ada_w.shape[0]
    nb = 1536
    return pl.pallas_call(
        _mod_kernel,
        out_shape=jax.ShapeDtypeStruct((L, 8, 6 * D_MODEL), _f32),
        grid=(L, 6 * D_MODEL // nb),
        in_specs=[
            pl.BlockSpec((8, D_MODEL), lambda l, j: (0, 0)),
            pl.BlockSpec((1, D_MODEL, nb), lambda l, j: (l, 0, j)),
            pl.BlockSpec((1, 1, nb), lambda l, j: (l, 0, j)),
        ],
        out_specs=pl.BlockSpec((1, 8, nb), lambda l, j: (l, 0, j)),
        compiler_params=pltpu.CompilerParams(
            dimension_semantics=("arbitrary", "arbitrary"), vmem_limit_bytes=VMEM_LIMIT),
        name="adaln_mod",
    )(c8, ada_w, ada_b)


def _in_proj_kernel(x_ref, mod_ref, wa_ref, ba_ref, wlat_ref, blat_ref, wb_ref, bb_ref,
                    gmg_ref, gmb_ref, ws_ref, bst_ref,
                    qg_ref, wuqt_ref, kvg_ref, wuk_ref, wuvt_ref, cosk_ref, sin_ref, cosqt_ref, sint_ref,
                    ya_ref, q_ref, k_ref, vt_ref, xb_ref, gg_ref, zg_ref):
    x = x_ref[0]
    h = (_ln(x) * mod_ref[0, 1:2, :] + mod_ref[0, 0:1, :]).astype(_bf16)

    def proj(w_ref, b_ref, lo, hi):
        return _dot(h, w_ref[:, lo:hi]) + b_ref[:, lo:hi]

    zg = jax.nn.gelu(proj(wa_ref, ba_ref, OFF_GM, OFF_QLAT))
    u = zg[:, :GM_WIDTH]
    v = (_ln(zg[:, GM_WIDTH:]) * gmg_ref[...] + gmb_ref[...]).astype(_bf16)
    row = lax.broadcasted_iota(jnp.int32, (GM_BLOCK, GM_BLOCK), 0) // CHUNK
    col = lax.broadcasted_iota(jnp.int32, (GM_BLOCK, GM_BLOCK), 1) // CHUNK
    gch = GM_WIDTH // GM_GROUPS
    for g in range(GM_GROUPS):
        wg = jnp.where(row >= col, ws_ref[g], 0.0).astype(_bf16)
        bias = bst_ref[:, g:g + 1]
        for r in range(TM // GM_BLOCK):
            rs = slice(r * GM_BLOCK, (r + 1) * GM_BLOCK)
            cs = slice(g * gch, (g + 1) * gch)
            f = _dot(wg, v[rs, cs]) + bias
            ya_ref[0, rs, cs] = (u[rs, cs] * f).astype(_bf16)

    zlat = proj(wlat_ref, blat_ref, 0, LAT_W)
    qn = _rms(zlat[:, :MLA_Q_RANK], qg_ref[...]).astype(_bf16)
    qscale = (MLA_NOPE + MLA_ROPE) ** -0.5 * LOG2E
    qmt = _dot_nt(wuqt_ref[...], qn)
    cos_t = cosqt_ref[...]
    sin_t = sint_ref[...]
    for hd in range(MLA_HEADS):
        hs = slice(hd * HEAD_PAD, (hd + 1) * HEAD_PAD)
        g = qmt[hs, :]
        rot = pltpu.roll(g, HEAD_PAD - MLA_ROPE, 0)
        q_ref[0, hs, :] = ((g * cos_t + rot * sin_t) * qscale).astype(_bf16)

    ckv = _rms(zlat[:, MLA_Q_RANK:MLA_Q_RANK + MLA_KV_RANK], kvg_ref[...]).astype(_bf16)
    gk = zlat[:, MLA_Q_RANK + MLA_KV_RANK:]
    kr = gk * cosk_ref[...] + pltpu.roll(gk, MLA_NOPE, 1) * sin_ref[...]
    kn = _dot(ckv, wuk_ref[...])
    nope_lane = lax.broadcasted_iota(jnp.int32, (TM, HEAD_PAD), 1) < MLA_NOPE
    for hd in range(MLA_HEADS):
        hs = slice(hd * HEAD_PAD, (hd + 1) * HEAD_PAD)
        pair = kn[:, (hd // 2) * HEAD_PAD:(hd // 2 + 1) * HEAD_PAD]
        if hd % 2:
            pair = pltpu.roll(pair, MLA_NOPE, 1)
        k_ref[0, :, hs] = jnp.where(nope_lane, pair, kr).astype(_bf16)
    vt_ref[0] = _dot_nt(wuvt_ref[...], ckv).astype(_bf16)

    xb_ref[0] = proj(wb_ref, bb_ref, 0, LRU_WIDTH)
    gg_ref[0] = jax.nn.gelu(proj(wb_ref, bb_ref, LRU_WIDTH, 2 * LRU_WIDTH)).astype(_bf16)
    for n in range(N_BRANCH):
        lo = 2 * LRU_WIDTH + n * D_MODEL
        zg_ref[0, :, n * D_MODEL:(n + 1) * D_MODEL] = proj(wb_ref, bb_ref, lo, lo + D_MODEL).astype(_bf16)


def _in_proj(l, x, mod, p, rope):
    B, S, D = x.shape
    tile = lambda w: pl.BlockSpec((1, TM, w), lambda b, i: (b, i, 0))
    outs = [
        jax.ShapeDtypeStruct((B, S, GM_WIDTH), _bf16),
        jax.ShapeDtypeStruct((B, QK_W, S), _bf16),
        jax.ShapeDtypeStruct((B, S, QK_W), _bf16),
        jax.ShapeDtypeStruct((B, MLA_HEADS * MLA_V, S), _bf16),
        jax.ShapeDtypeStruct((B, S, LRU_WIDTH), _f32),
        jax.ShapeDtypeStruct((B, S, LRU_WIDTH), _bf16),
        jax.ShapeDtypeStruct((B, S, N_BRANCH * D_MODEL), _bf16),
    ]
    return pl.pallas_call(
        _in_proj_kernel,
        out_shape=outs,
        grid=(B, S // TM),
        in_specs=[
            tile(D),
            _mod_spec(l),
            _layer_spec(l, (D, OFF_QLAT)), _layer_spec(l, (1, OFF_QLAT)),
            _layer_spec(l, (D, LAT_W)), _layer_spec(l, (1, LAT_W)),
            _layer_spec(l, (D, N_IN - OFF_LRU_X)), _layer_spec(l, (1, N_IN - OFF_LRU_X)),
            _layer_spec(l, (1, GM_WIDTH)),
            _layer_spec(l, (1, GM_WIDTH)),
            _layer_spec(l, (GM_GROUPS, GM_BLOCK, GM_BLOCK)),
            _layer_spec(l, (GM_BLOCK, GM_GROUPS)),
            _layer_spec(l, (1, MLA_Q_RANK)),
            _layer_spec(l, (QK_W, MLA_Q_RANK)),
            _layer_spec(l, (1, MLA_KV_RANK)),
            _layer_spec(l, (MLA_KV_RANK, MLA_HEADS * MLA_NOPE)),
            _layer_spec(l, (MLA_HEADS * MLA_V, MLA_KV_RANK)),
            pl.BlockSpec((TM, HEAD_PAD), lambda b, i: (i, 0)),
            pl.BlockSpec((TM, HEAD_PAD), lambda b, i: (i, 0)),
            pl.BlockSpec((HEAD_PAD, TM), lambda b, i: (0, i)),
            pl.BlockSpec((HEAD_PAD, TM), lambda b, i: (0, i)),
        ],
        out_specs=[
            tile(GM_WIDTH),
            pl.BlockSpec((1, QK_W, TM), lambda b, i: (b, 0, i)),
            tile(QK_W),
            pl.BlockSpec((1, MLA_HEADS * MLA_V, TM), lambda b, i: (b, 0, i)),
            tile(LRU_WIDTH), tile(LRU_WIDTH), tile(N_BRANCH * D_MODEL),
        ],
        compiler_params=pltpu.CompilerParams(
            dimension_semantics=("arbitrary", "arbitrary"), vmem_limit_bytes=VMEM_LIMIT),
        name="in_proj",
    )(x, mod, p["w_a"], p["b_a"], p["w_lat"], p["b_lat"], p["w_b"], p["b_b"],
      p["gm_g"], p["gm_b"], p["gm_ws"], p["gm_bst"],
      p["q_g"], p["wuqt"], p["kv_g"], p["wuk"], p["wuvt"], *rope)


SUB = 256
ACC_ROWS = MLA_V + 16
LAG_STATS, LAG_PV, LAG_ACC = 3, 5, 8


def _attn_units(rel):
    units = []
    for kh in range(TK // SUB):
        for hd in range(MLA_HEADS):
            for c in range(TQ // SUB):
                mode = "full"
                if rel is not None:
                    koff, qoff = rel + kh * SUB, c * SUB
                    if koff >= qoff + SUB:
                        continue
                    if koff + SUB > qoff:
                        assert koff == qoff
                        mode = "diag"
                units.append((kh, hd, c, mode))
    return units


def _attn_kernel(qi_ref, ki_ref, q_ref, k_ref, vt_ref, o_ref, m_sc, acc_sc, o_sc):
    step = pl.program_id(1)
    qi = qi_ref[step]
    ki = ki_ref[step]
    nsub = SUB // 8

    @pl.when(ki == 0)
    def _():
        m_sc[...] = jnp.full(m_sc.shape, NEG_BIG, _f32)
        acc_sc[...] = jnp.zeros(acc_sc.shape, _f32)

    def stage_qk(unit, st, visible):
        kh, hd, c, mode = unit
        hs = slice(hd * HEAD_PAD, (hd + 1) * HEAD_PAD)
        s = _dot(k_ref[0, kh * SUB:(kh + 1) * SUB, hs], q_ref[0, hs, c * SUB:(c + 1) * SUB])
        if mode == "diag":
            s = jnp.where(visible, s, NEG_BIG)
        st["s3"] = s.reshape(nsub, 8, SUB)

    def stage_stats(unit, st, visible):
        kh, hd, c, mode = unit
        cm = jnp.max(st["s3"], axis=0)
        for d in (4, 2, 1):
            cm = jnp.maximum(cm, pltpu.roll(cm, d, 0))
        msl = (slice(hd * 8, (hd + 1) * 8), slice(c * SUB, (c + 1) * SUB))
        m_old = m_sc[msl]
        m_new = jnp.maximum(m_old, cm)
        m_sc[msl] = m_new
        st["alpha"] = jnp.exp2(m_old - m_new)
        st["p"] = jnp.exp2(st.pop("s3") - m_new[None]).reshape(SUB, SUB).astype(_bf16)

    def stage_pv(unit, st, visible):
        kh, hd, c, _ = unit
        v_t = vt_ref[0, hd * MLA_V:(hd + 1) * MLA_V, kh * SUB:(kh + 1) * SUB]
        v_ext = jnp.concatenate([v_t, jnp.ones((ACC_ROWS - MLA_V, SUB), _bf16)], axis=0)
        st["pv"] = _dot(v_ext, st.pop("p")).reshape(ACC_ROWS // 8, 8, SUB)

    def stage_acc(unit, st, visible):
        kh, hd, c, _ = unit
        cs = slice(c * SUB, (c + 1) * SUB)
        acc = acc_sc[hd, :, cs].reshape(ACC_ROWS // 8, 8, SUB)
        acc_sc[hd, :, cs] = (st.pop("alpha")[None] * acc + st.pop("pv")).reshape(ACC_ROWS, SUB)

    stages = ((stage_qk, 0), (stage_stats, LAG_STATS), (stage_pv, LAG_PV), (stage_acc, LAG_ACC))

    def update(rel):
        units = _attn_units(rel)
        visible = None
        if rel is not None:
            row = lax.broadcasted_iota(jnp.int32, (SUB, SUB), 0) // CHUNK
            col = lax.broadcasted_iota(jnp.int32, (SUB, SUB), 1) // CHUNK
            visible = row <= col
        n = len(units)
        state = [{} for _ in units]
        for t in range(n + LAG_ACC):
            for fn, lag in stages:
                if 0 <= t - lag < n:
                    fn(units[t - lag], state[t - lag], visible)

    rel = ki * TK - qi * TQ
    for r in range(0, TQ, TK):
        pl.when(rel == r)(functools.partial(update, r))
    pl.when(rel < 0)(functools.partial(update, None))

    @pl.when(rel == TQ - TK)
    def _():
        for hd in range(MLA_HEADS):
            inv = 1.0 / acc_sc[hd, MLA_V:MLA_V + 8, :]
            o = acc_sc[hd, 0:MLA_V, :].reshape(MLA_V // 8, 8, TQ) * inv[None]
            o_sc[hd * MLA_V:(hd + 1) * MLA_V, :] = o.reshape(MLA_V, TQ)
        o_ref[0] = o_sc[...].T.astype(_bf16)


def _attention(qt, k, vt):
    B, _, S = qt.shape
    nq = S // TQ
    ratio = TQ // TK
    qi_tab, ki_tab = [], []
    for i in range(nq):
        for j in range((i + 1) * ratio):
            qi_tab.append(i)
            ki_tab.append(j)
    n_steps = len(qi_tab)
    qi_tab = jnp.asarray(np.array(qi_tab, np.int32))
    ki_tab = jnp.asarray(np.array(ki_tab, np.int32))
    grid_spec = pltpu.PrefetchScalarGridSpec(
        num_scalar_prefetch=2,
        grid=(B, n_steps),
        in_specs=[
            pl.BlockSpec((1, QK_W, TQ), lambda b, s, qi, ki: (b, 0, qi[s])),
            pl.BlockSpec((1, TK, QK_W), lambda b, s, qi, ki: (b, ki[s], 0)),
            pl.BlockSpec((1, MLA_HEADS * MLA_V, TK), lambda b, s, qi, ki: (b, 0, ki[s])),
        ],
        out_specs=pl.BlockSpec((1, TQ, MLA_HEADS * MLA_V), lambda b, s, qi, ki: (b, qi[s], 0)),
        scratch_shapes=[
            pltpu.VMEM((MLA_HEADS * 8, TQ), _f32),
            pltpu.VMEM((MLA_HEADS, ACC_ROWS, TQ), _f32),
            pltpu.VMEM((MLA_HEADS * MLA_V, TQ), _f32),
        ],
    )
    return pl.pallas_call(
        _attn_kernel,
        out_shape=jax.ShapeDtypeStruct((B, S, MLA_HEADS * MLA_V), _bf16),
        grid_spec=grid_spec,
        compiler_params=pltpu.CompilerParams(
            dimension_semantics=("arbitrary", "arbitrary"), vmem_limit_bytes=VMEM_LIMIT),
        name="mla_attn",
    )(qi_tab, ki_tab, qt, k, vt)


SEG = TS // 8
SEG_PITCH = SEG + 8
LRU_SLABS = LRU_WIDTH // 128


def _rglru_kernel(xb_ref, gg_ref, cw_ref, cb_ref, wr_ref, br_ref, wi_ref, bi_ref, lam_ref,
                  y_ref, seg_sc, hl_sc, cp_sc, tail_sc, carry_sc):
    i = pl.program_id(1)

    @pl.when(i == 0)
    def _():
        tail_sc[...] = jnp.zeros(tail_sc.shape, _f32)
        carry_sc[...] = jnp.zeros(carry_sc.shape, _f32)

    for sl in range(LRU_SLABS):
        for r in range(8):
            seg_sc[sl, r * SEG_PITCH:r * SEG_PITCH + SEG, :] = (
                xb_ref[0, r * SEG:(r + 1) * SEG, sl * 128:(sl + 1) * 128])

    rows = lax.broadcasted_iota(jnp.int32, (8, 128), 0)
    xc_slabs = []
    for sl in range(LRU_SLABS):
        ls = slice(sl * 128, (sl + 1) * 128)
        xg = [seg_sc.at[sl][pl.ds(g, 8, stride=SEG_PITCH), :] for g in range(SEG)]
        before = [jnp.where(rows == 0, tail_sc[j - 1, :, ls], pltpu.roll(xg[SEG - j], 1, 0))
                  for j in range(1, CONV_W)]
        xs = before[::-1] + xg
        xc_g = []
        for g in range(SEG):
            acc = cb_ref[:, ls] + xs[g + CONV_W - 1] * cw_ref[CONV_W - 1:CONV_W, ls]
            for j in range(1, CONV_W):
                acc = acc + xs[g + CONV_W - 1 - j] * cw_ref[CONV_W - 1 - j:CONV_W - j, ls]
            xc_g.append(acc)
        for j in range(1, CONV_W):
            tail_sc[j - 1, :, ls] = jnp.broadcast_to(xg[SEG - j][7:8, :], (8, 128))
        xc_slabs.append(jnp.concatenate(xc_g, axis=0))
    xc = jnp.concatenate(xc_slabs, axis=1)

    xcb = xc.astype(_bf16)
    th_r = jnp.tanh(_dot(xcb, wr_ref[...]) + br_ref[...])
    th_i = jnp.tanh(_dot(xcb, wi_ref[...]) + bi_ref[...])
    lam = lam_ref[...]
    softplus = jnp.maximum(-lam, 0.0) + jnp.log1p(jnp.exp(-jnp.abs(lam)))
    half = (0.5 * LRU_C) * softplus
    nla = th_r * half + half
    a = jnp.exp(-nla)
    z = jnp.tanh(nla) * (a * a + 1.0)
    gain = jnp.where(z > 0.0, z * lax.rsqrt(z), 0.0)
    b = gain * ((0.5 * th_i + 0.5) * xc)

    h = [None] * LRU_SLABS
    cp = [None] * LRU_SLABS
    for g in range(SEG):
        for sl in range(LRU_SLABS):
            ls = slice(sl * 128, (sl + 1) * 128)
            a_g = a[g * 8:(g + 1) * 8, ls]
            b_g = b[g * 8:(g + 1) * 8, ls]
            h[sl] = b_g if g == 0 else a_g * h[sl] + b_g
            cp[sl] = a_g if g == 0 else a_g * cp[sl]
            hl_sc[g, :, ls] = h[sl]
            cp_sc[g, :, ls] = cp[sl]

    for sl in range(LRU_SLABS):
        ls = slice(sl * 128, (sl + 1) * 128)
        a8 = jnp.where(rows == 0, 0.0, pltpu.roll(cp[sl], 1, 0))
        b8 = jnp.where(rows == 0, carry_sc[:, ls], pltpu.roll(h[sl], 1, 0))
        for d in (1, 2, 4):
            ok = rows >= d
            a_sh = pltpu.roll(a8, d, 0)
            b_sh = pltpu.roll(b8, d, 0)
            b8 = jnp.where(ok, a8 * b_sh + b8, b8)
            a8 = jnp.where(ok, a8 * a_sh, a8)
        inflow = b8
        for g in range(SEG):
            hf = hl_sc[g, :, ls] + cp_sc[g, :, ls] * inflow
            seg_sc.at[sl][pl.ds(g, 8, stride=SEG_PITCH), :] = hf
        carry_sc[:, ls] = jnp.broadcast_to(hf[7:8, :], (8, 128))
        for r in range(8):
            rs = slice(r * SEG, (r + 1) * SEG)
            hr = seg_sc[sl, r * SEG_PITCH:r * SEG_PITCH + SEG, :]
            y_ref[0, rs, ls] = (hr * gg_ref[0, rs, ls].astype(_f32)).astype(_bf16)


def _rglru(l, xb, gg, p):
    B, S, W = xb.shape
    tile = pl.BlockSpec((1, TS, W), lambda b, i: (b, i, 0))
    return pl.pallas_call(
        _rglru_kernel,
        out_shape=jax.ShapeDtypeStruct((B, S, W), _bf16),
        grid=(B, S // TS),
        in_specs=[
            tile, tile,
            _layer_spec(l, (CONV_W, W)), _layer_spec(l, (1, W)),
            _layer_spec(l, (W, W)), _layer_spec(l, (1, W)),
            _layer_spec(l, (W, W)), _layer_spec(l, (1, W)),
            _layer_spec(l, (1, W)),
        ],
        out_specs=tile,
        scratch_shapes=[
            pltpu.VMEM((LRU_SLABS, 8 * SEG_PITCH, 128), _f32),
            pltpu.VMEM((SEG, 8, W), _f32),
            pltpu.VMEM((SEG, 8, W), _f32),
            pltpu.VMEM((CONV_W - 1, 8, W), _f32),
            pltpu.VMEM((8, W), _f32),
        ],
        compiler_params=pltpu.CompilerParams(
            dimension_semantics=("arbitrary", "arbitrary"), vmem_limit_bytes=VMEM_LIMIT),
        name="rglru",
    )(xb, gg, p["conv_w"], p["conv_b"], p["wr"], p["br"], p["wi"], p["bi"], p["lam"])


def _merge_kernel(x_ref, mod_ref, ya_ref, yb_ref, yc_ref, zg_ref, bw_ref, mw_ref, lng_ref,
                  lnb_ref, o_ref):
    merged = None
    for n, y_ref in enumerate((ya_ref, yb_ref, yc_ref)):
        gate = _sigmoid(zg_ref[0, :, n * D_MODEL:(n + 1) * D_MODEL].astype(_f32))
        t = gate * _dot(y_ref[0], bw_ref[n])
        merged = t if merged is None else merged + t
    mix = _dot(merged.astype(_bf16), mw_ref[...])
    r = ALPHA * x_ref[0] + mod_ref[0, 2:3, :] * mix
    o_ref[0] = _ln(r) * lng_ref[...] + lnb_ref[...]


def _merge(l, x, mod, ya, yb, yc, zg, p):
    B, S, D = x.shape
    tile = lambda w: pl.BlockSpec((1, TM, w), lambda b, i: (b, i, 0))
    return pl.pallas_call(
        _merge_kernel,
        out_shape=jax.ShapeDtypeStruct((B, S, D), _f32),
        grid=(B, S // TM),
        in_specs=[
            tile(D),
            _mod_spec(l),
            tile(GM_WIDTH), tile(GM_WIDTH), tile(GM_WIDTH), tile(N_BRANCH * D),
            _layer_spec(l, (N_BRANCH, GM_WIDTH, D)),
            _layer_spec(l, (D, D)),
            _layer_spec(l, (1, D)), _layer_spec(l, (1, D)),
        ],
        out_specs=tile(D),
        compiler_params=pltpu.CompilerParams(
            dimension_semantics=("arbitrary", "arbitrary"), vmem_limit_bytes=VMEM_LIMIT),
        name="merge",
    )(x, mod, ya, yb, yc, zg, p["branch_w"], p["mix_w"], p["ln_g0"], p["ln_b0"])


FF_CHUNK = 1024
FF_TM = 1024
FF_SUB = 512


def _ffn_kernel(x_ref, mod_ref, w1_ref, b1_ref, w2_ref, b2_ref, lng_ref, lnb_ref, o_ref):
    nsub = FF_TM // FF_SUB
    nchunk = D_FF // FF_CHUNK
    hs = [None] * nsub
    fs = [None] * nsub

    def rows(j):
        return slice(j * FF_SUB, (j + 1) * FF_SUB)

    def pre(j):
        hs[j] = (_ln(x_ref[0, rows(j), :]) * mod_ref[0, 4:5, :] + mod_ref[0, 3:4, :]).astype(_bf16)

    def mm(j, c):
        cs = slice(c * FF_CHUNK, (c + 1) * FF_CHUNK)
        a = jnp.maximum(_dot(hs[j], w1_ref[:, cs]) + b1_ref[:, cs], 0.0)
        t = _dot((a * a).astype(_bf16), w2_ref[cs, :])
        fs[j] = t if c == 0 else fs[j] + t

    def post(j):
        r = ALPHA * x_ref[0, rows(j), :] + mod_ref[0, 5:6, :] * (fs[j] + b2_ref[...])
        o_ref[0, rows(j), :] = _ln(r) * lng_ref[...] + lnb_ref[...]

    pre(0)
    for j in range(nsub):
        for c in range(nchunk):
            mm(j, c)
            if c == 0 and j + 1 < nsub:
                pre(j + 1)
            if c == 1 and j > 0:
                post(j - 1)
    post(nsub - 1)


def _ffn(l, x, mod, p):
    B, S, D = x.shape
    tile = pl.BlockSpec((1, FF_TM, D), lambda b, i: (b, i, 0))
    return pl.pallas_call(
        _ffn_kernel,
        out_shape=jax.ShapeDtypeStruct((B, S, D), _f32),
        grid=(B, S // FF_TM),
        in_specs=[
            tile,
            _mod_spec(l),
            _layer_spec(l, (D, D_FF)), _layer_spec(l, (1, D_FF)),
            _layer_spec(l, (D_FF, D)), _layer_spec(l, (1, D)),
            _layer_spec(l, (1, D)), _layer_spec(l, (1, D)),
        ],
        out_specs=tile,
        compiler_params=pltpu.CompilerParams(
            dimension_semantics=("arbitrary", "arbitrary"), vmem_limit_bytes=VMEM_LIMIT),
        name="ffn",
    )(x, mod, p["w1"], p["b1"], p["w2"], p["b2"], p["ln_g1"], p["ln_b1"])


def _swap_halves(w):
    half = w.shape[-1] // 2
    return jnp.concatenate([w[..., half:], w[..., :half]], axis=-1)


def _block_diag(w):
    n, c, d = w.shape[-3:]
    eye = jnp.eye(n, dtype=w.dtype)
    dense = w[..., :, :, None, :] * eye[:, None, :, None]
    return dense.reshape(w.shape[:-3] + (n * c, n * d))


def _prepare_params(in_w, in_b, gm_ln_g, gm_ln_b, gm_ws, gm_bs, mla_qnorm_g, mla_wuq,
                    mla_kvnorm_g, mla_wukv, lru_conv_w, lru_conv_b, lru_wr, lru_br, lru_wi,
                    lru_bi, lru_lambda, branch_w, mix_out_w, ffn_w1, ffn_b1, ffn_w2, ffn_b2,
                    ln_g, ln_b):
    L = in_w.shape[0]
    in_b = in_b[:, None, :]

    def lat_cols(t):
        kr = t[..., OFF_KROPE:OFF_LRU_X]
        z32 = jnp.zeros(kr.shape[:-1] + (MLA_ROPE,), t.dtype)
        return jnp.concatenate([t[..., OFF_QLAT:OFF_KROPE], _swap_halves(kr), z32, kr, z32], axis=-1)

    in_w = in_w.astype(_bf16)
    wuq = mla_wuq.reshape(L, MLA_Q_RANK, MLA_HEADS, MLA_NOPE + MLA_ROPE)
    wuq = jnp.concatenate([wuq, _swap_halves(wuq[..., MLA_NOPE:])], axis=-1)
    wukv = mla_wukv.reshape(L, MLA_KV_RANK, MLA_HEADS, MLA_NOPE + MLA_V)
    wuk = wukv[..., :MLA_NOPE].reshape(L, MLA_KV_RANK, MLA_HEADS * MLA_NOPE)
    wuvt = wukv[..., MLA_NOPE:].reshape(L, MLA_KV_RANK, MLA_HEADS * MLA_V).transpose(0, 2, 1)
    row = lambda v: v.reshape(L, 1, -1)
    return dict(
        w_a=in_w[:, :, :OFF_QLAT], b_a=in_b[:, :, :OFF_QLAT],
        w_lat=lat_cols(in_w), b_lat=lat_cols(in_b),
        w_b=in_w[:, :, OFF_LRU_X:], b_b=in_b[:, :, OFF_LRU_X:],
        gm_g=row(gm_ln_g), gm_b=row(gm_ln_b),
        gm_ws=gm_ws, gm_bst=gm_bs.transpose(0, 2, 1),
        q_g=row(mla_qnorm_g),
        wuqt=wuq.reshape(L, MLA_Q_RANK, QK_W).transpose(0, 2, 1).astype(_bf16),
        kv_g=row(mla_kvnorm_g),
        wuk=wuk.astype(_bf16), wuvt=wuvt.astype(_bf16),
        conv_w=lru_conv_w, conv_b=row(lru_conv_b),
        wr=(0.5 * _block_diag(lru_wr)).astype(_bf16), br=0.5 * row(lru_br),
        wi=(0.5 * _block_diag(lru_wi)).astype(_bf16), bi=0.5 * row(lru_bi),
        lam=row(lru_lambda),
        branch_w=branch_w.astype(_bf16), mix_w=mix_out_w.astype(_bf16),
        w1=ffn_w1.astype(_bf16), b1=row(ffn_b1),
        w2=ffn_w2.astype(_bf16), b2=row(ffn_b2),
        ln_g0=row(ln_g[:, 0]), ln_b0=row(ln_b[:, 0]),
        ln_g1=row(ln_g[:, 1]), ln_b1=row(ln_b[:, 1]),
    )


def _rope_tables(seq):
    pos = jnp.arange(seq, dtype=_f32)
    inv = ROPE_BASE ** (-jnp.arange(0, MLA_ROPE, 2, dtype=_f32) / MLA_ROPE)
    ang = pos[:, None] * inv[None, :]
    cos, sin = jnp.cos(ang), jnp.sin(ang)
    ones = jnp.ones((seq, MLA_NOPE), _f32)
    zeros = jnp.zeros((seq, MLA_NOPE), _f32)
    pad = jnp.zeros((seq, HEAD_PAD - MLA_NOPE - MLA_ROPE), _f32)
    cos_q = jnp.concatenate([ones, cos, cos, pad], axis=1)
    cos_k = jnp.concatenate([zeros, cos, cos, pad], axis=1)
    sin_t = jnp.concatenate([zeros, -sin, sin, pad], axis=1)
    return cos_k, sin_t, cos_q.T, sin_t.T


def kernel(x, c, ada_w, ada_b, in_w, in_b, gm_ln_g, gm_ln_b, gm_ws, gm_bs, mla_qnorm_g, mla_wuq, mla_kvnorm_g, mla_wukv, lru_conv_w, lru_conv_b, lru_wr, lru_br, lru_wi, lru_bi, lru_lambda, branch_w, mix_out_w, ffn_w1, ffn_b1, ffn_w2, ffn_b2, ln_g, ln_b):
    B, S, D = x.shape
    assert (B, S, D) == (c.shape[0], S, D_MODEL) and S % TQ == 0 and S % TM == 0 and S % FF_TM == 0
    rope = _rope_tables(S)
    p = _prepare_params(in_w, in_b, gm_ln_g, gm_ln_b, gm_ws, gm_bs, mla_qnorm_g, mla_wuq,
                        mla_kvnorm_g, mla_wukv, lru_conv_w, lru_conv_b, lru_wr, lru_br, lru_wi,
                        lru_bi, lru_lambda, branch_w, mix_out_w, ffn_w1, ffn_b1, ffn_w2, ffn_b2,
                        ln_g, ln_b)
    c8 = jnp.concatenate([c, jnp.zeros((8 - B, D), c.dtype)], axis=0)
    mod_all = _modulation(c8, ada_w, ada_b.reshape(DEPTH, 1, 6 * D))
    m6 = mod_all[:, :B].reshape(DEPTH, B, 6, D)
    m6 = m6 + jnp.array([0.0, 1.0, 1.0, 0.0, 1.0, 1.0], _f32).reshape(1, 1, 6, 1)
    mod = jnp.concatenate([m6, jnp.zeros((DEPTH, B, 2, D), _f32)], axis=2)
    for l in range(DEPTH):
        ya, q, k, vt, xb, gg, zg = _in_proj(l, x, mod, p, rope)
        yb = _attention(q, k, vt)
        yc = _rglru(l, xb, gg, p)
        x = _merge(l, x, mod, ya, yb, yc, zg, p)
        x = _ffn(l, x, mod, p)
    return x
```

```python
import functools
import math

import jax
import jax.numpy as jnp
import numpy as np
from jax import lax
from jax.experimental import pallas as pl
from jax.experimental.pallas import tpu as pltpu

D_MODEL = 1024
DEPTH = 2
CHUNK = 64
GM_BLOCK = 128
GM_WIDTH = 512
GM_GROUPS = 4
MLA_HEADS = 8
MLA_Q_RANK = 256
MLA_KV_RANK = 128
MLA_NOPE = 64
MLA_ROPE = 32
MLA_V = 64
ROPE_BASE = 10000.0
LRU_WIDTH = 512
LRU_BLOCKS = 8
CONV_W = 4
LRU_C = 8.0
N_BRANCH = 3
D_FF = 4 * D_MODEL
ALPHA = (2.0 * DEPTH) ** 0.25
LN_EPS = 1e-5
RMS_EPS = 1e-6

OFF_QLAT = 2 * GM_WIDTH
OFF_KVLAT = OFF_QLAT + MLA_Q_RANK
OFF_KROPE = OFF_KVLAT + MLA_KV_RANK
OFF_LRU_X = OFF_KROPE + MLA_ROPE
OFF_LRU_G = OFF_LRU_X + LRU_WIDTH
OFF_GATE = OFF_LRU_G + LRU_WIDTH
N_IN = OFF_GATE + N_BRANCH * D_MODEL

HEAD_PAD = 128
QK_W = MLA_HEADS * HEAD_PAD
OFF_GM = 0
GATE_CHUNKS = 6
LAT_W = MLA_Q_RANK + MLA_KV_RANK + HEAD_PAD

TM = 512
TQ = 1024
TK = 1024
TS = 512
VMEM_LIMIT = 56 * 1024 * 1024
NEG_BIG = -1e30
LOG2E = math.log2(math.e)

_bf16 = jnp.bfloat16
_f32 = jnp.float32


def _ln(x):
    mu = jnp.mean(x, axis=-1, keepdims=True)
    xc = x - mu
    var = jnp.mean(xc * xc, axis=-1, keepdims=True)
    return xc * lax.rsqrt(var + LN_EPS)


def _rms(x, g):
    return x * lax.rsqrt(jnp.mean(x * x, axis=-1, keepdims=True) + RMS_EPS) * g


def _sigmoid(x):
    return 0.5 * jnp.tanh(0.5 * x) + 0.5


def _dot(a, b):
    return jnp.dot(a, b, preferred_element_type=_f32)


def _dot_nt(a, b):
    return lax.dot_general(a, b, (((1,), (1,)), ((), ())), preferred_element_type=_f32)


def _mod_spec(l):
    return pl.BlockSpec((None, 1, 8, D_MODEL), lambda b, i: (l, b, 0, 0))


def _layer_spec(l, shape):
    nd = len(shape)
    return pl.BlockSpec((None,) + tuple(shape), lambda *_: (l,) + (0,) * nd,
                        pipeline_mode=pl.Buffered(1))


def _mod_kernel(c_ref, w_ref, b_ref, o_ref):
    c = c_ref[...]
    c_act = (c * jax.nn.sigmoid(c)).astype(_bf16)
    o_ref[0] = _dot(c_act, w_ref[0].astype(_bf16)) + b_ref[0]


def _modulation(c8, ada_w, ada_b):
    L = ada_w.shape[0]
    nb = 1536
    return pl.pallas_call(
        _mod_kernel,
        out_shape=jax.ShapeDtypeStruct((L, 8, 6 * D_MODEL), _f32),
        grid=(L, 6 * D_MODEL // nb),
        in_specs=[
            pl.BlockSpec((8, D_MODEL), lambda l, j: (0, 0)),
            pl.BlockSpec((1, D_MODEL, nb), lambda l, j: (l, 0, j)),
            pl.BlockSpec((1, 1, nb), lambda l, j: (l, 0, j)),
        ],
        out_specs=pl.BlockSpec((1, 8, nb), lambda l, j: (l, 0, j)),
        compiler_params=pltpu.CompilerParams(
            dimension_semantics=("arbitrary", "arbitrary"), vmem_limit_bytes=VMEM_LIMIT),
        name="adaln_mod",
    )(c8, ada_w, ada_b)


def _in_proj_kernel(x_ref, mod_ref, wa_ref, ba_ref, wlat_ref, blat_ref, wb_ref, bb_ref,
                    gmg_ref, gmb_ref, ws_ref, bst_ref,
                    qg_ref, wuqt_ref, kvg_ref, wuk_ref, wuvt_ref, cosk_ref, sin_ref, cosqt_ref, sint_ref,
                    ya_ref, q_ref, k_ref, vt_ref, xb_ref, gg_ref, zg_ref):
    x = x_ref[0]
    h = (_ln(x) * mod_ref[0, 1:2, :] + mod_ref[0, 0:1, :]).astype(_bf16)

    def proj(w_ref, b_ref, lo, hi):
        return _dot(h, w_ref[:, lo:hi]) + b_ref[:, lo:hi]

    gate_w = N_BRANCH * D_MODEL // GATE_CHUNKS

    def gate_chunk(i):
        lo = 2 * LRU_WIDTH + i * gate_w
        zg_ref[0, :, i * gate_w:(i + 1) * gate_w] = proj(wb_ref, bb_ref, lo, lo + gate_w).astype(_bf16)

    zgm = proj(wa_ref, ba_ref, OFF_GM, OFF_QLAT)
    gate_chunk(0)
    zg = jax.nn.gelu(zgm)
    u = zg[:, :GM_WIDTH]
    v = (_ln(zg[:, GM_WIDTH:]) * gmg_ref[...] + gmb_ref[...]).astype(_bf16)
    gate_chunk(1)
    row = lax.broadcasted_iota(jnp.int32, (GM_BLOCK, GM_BLOCK), 0) // CHUNK
    col = lax.broadcasted_iota(jnp.int32, (GM_BLOCK, GM_BLOCK), 1) // CHUNK
    gch = GM_WIDTH // GM_GROUPS
    for g in range(GM_GROUPS):
        wg = jnp.where(row >= col, ws_ref[g], 0.0).astype(_bf16)
        bias = bst_ref[:, g:g + 1]
        for r in range(TM // GM_BLOCK):
            rs = slice(r * GM_BLOCK, (r + 1) * GM_BLOCK)
            cs = slice(g * gch, (g + 1) * gch)
            f = _dot(wg, v[rs, cs]) + bias
            ya_ref[0, rs, cs] = (u[rs, cs] * f).astype(_bf16)
    gate_chunk(2)

    zlat = proj(wlat_ref, blat_ref, 0, LAT_W)
    gate_chunk(3)
    qn = _rms(zlat[:, :MLA_Q_RANK], qg_ref[...]).astype(_bf16)
    qscale = (MLA_NOPE + MLA_ROPE) ** -0.5 * LOG2E
    qmt = _dot_nt(wuqt_ref[...], qn)
    cos_t = cosqt_ref[...]
    sin_t = sint_ref[...]
    for hd in range(MLA_HEADS):
        hs = slice(hd * HEAD_PAD, (hd + 1) * HEAD_PAD)
        g = qmt[hs, :]
        rot = pltpu.roll(g, HEAD_PAD - MLA_ROPE, 0)
        q_ref[0, hs, :] = ((g * cos_t + rot * sin_t) * qscale).astype(_bf16)
    gate_chunk(4)

    ckv = _rms(zlat[:, MLA_Q_RANK:MLA_Q_RANK + MLA_KV_RANK], kvg_ref[...]).astype(_bf16)
    gk = zlat[:, MLA_Q_RANK + MLA_KV_RANK:]
    kr = gk * cosk_ref[...] + pltpu.roll(gk, MLA_NOPE, 1) * sin_ref[...]
    kn = _dot(ckv, wuk_ref[...])
    for hd in range(MLA_HEADS):
        hs = slice(hd * HEAD_PAD, (hd + 1) * HEAD_PAD)
        k_ref[0, :, hs] = (kn[:, hs] + kr).astype(_bf16)
    vt_ref[0] = _dot_nt(wuvt_ref[...], ckv).astype(_bf16)
    gate_chunk(5)

    xb_ref[0] = proj(wb_ref, bb_ref, 0, LRU_WIDTH)
    gg_ref[0] = jax.nn.gelu(proj(wb_ref, bb_ref, LRU_WIDTH, 2 * LRU_WIDTH)).astype(_bf16)


def _in_proj(l, x, mod, p, rope):
    B, S, D = x.shape
    tile = lambda w: pl.BlockSpec((1, TM, w), lambda b, i: (b, i, 0))
    outs = [
        jax.ShapeDtypeStruct((B, S, GM_WIDTH), _bf16),
        jax.ShapeDtypeStruct((B, QK_W, S), _bf16),
        jax.ShapeDtypeStruct((B, S, QK_W), _bf16),
        jax.ShapeDtypeStruct((B, MLA_HEADS * MLA_V, S), _bf16),
        jax.ShapeDtypeStruct((B, S, LRU_WIDTH), _f32),
        jax.ShapeDtypeStruct((B, S, LRU_WIDTH), _bf16),
        jax.ShapeDtypeStruct((B, S, N_BRANCH * D_MODEL), _bf16),
    ]
    return pl.pallas_call(
        _in_proj_kernel,
        out_shape=outs,
        grid=(B, S // TM),
        in_specs=[
            tile(D),
            _mod_spec(l),
            _layer_spec(l, (D, OFF_QLAT)), _layer_spec(l, (1, OFF_QLAT)),
            _layer_spec(l, (D, LAT_W)), _layer_spec(l, (1, LAT_W)),
            _layer_spec(l, (D, N_IN - OFF_LRU_X)), _layer_spec(l, (1, N_IN - OFF_LRU_X)),
            _layer_spec(l, (1, GM_WIDTH)),
            _layer_spec(l, (1, GM_WIDTH)),
            _layer_spec(l, (GM_GROUPS, GM_BLOCK, GM_BLOCK)),
            _layer_spec(l, (GM_BLOCK, GM_GROUPS)),
            _layer_spec(l, (1, MLA_Q_RANK)),
            _layer_spec(l, (QK_W, MLA_Q_RANK)),
            _layer_spec(l, (1, MLA_KV_RANK)),
            _layer_spec(l, (MLA_KV_RANK, QK_W)),
            _layer_spec(l, (MLA_HEADS * MLA_V, MLA_KV_RANK)),
            pl.BlockSpec((TM, HEAD_PAD), lambda b, i: (i, 0)),
            pl.BlockSpec((TM, HEAD_PAD), lambda b, i: (i, 0)),
            pl.BlockSpec((HEAD_PAD, TM), lambda b, i: (0, i)),
            pl.BlockSpec((HEAD_PAD, TM), lambda b, i: (0, i)),
        ],
        out_specs=[
            tile(GM_WIDTH),
            pl.BlockSpec((1, QK_W, TM), lambda b, i: (b, 0, i)),
            tile(QK_W),
            pl.BlockSpec((1, MLA_HEADS * MLA_V, TM), lambda b, i: (b, 0, i)),
            tile(LRU_WIDTH), tile(LRU_WIDTH), tile(N_BRANCH * D_MODEL),
        ],
        compiler_params=pltpu.CompilerParams(
            dimension_semantics=("arbitrary", "arbitrary"), vmem_limit_bytes=VMEM_LIMIT),
        name="in_proj",
    )(x, mod, p["w_a"], p["b_a"], p["w_lat"], p["b_lat"], p["w_b"], p["b_b"],
      p["gm_g"], p["gm_b"], p["gm_ws"], p["gm_bst"],
      p["q_g"], p["wuqt"], p["kv_g"], p["wuk"], p["wuvt"], *rope)


SUB = 256
ACC_ROWS = MLA_V + 16
LAG_STATS, LAG_PV, LAG_ACC = 3, 5, 8


def _attn_units(rel):
    units = []
    for kh in range(TK // SUB):
        for hd in range(MLA_HEADS):
            for c in range(TQ // SUB):
                mode = "full"
                if rel is not None:
                    koff, qoff = rel + kh * SUB, c * SUB
                    if koff >= qoff + SUB:
                        continue
                    if koff + SUB > qoff:
                        assert koff == qoff
                        mode = "diag"
                units.append((kh, hd, c, mode))
    return units


def _attn_kernel(qi_ref, ki_ref, q_ref, k_ref, vt_ref, o_ref, m_sc, acc_sc, o_sc):
    step = pl.program_id(1)
    qi = qi_ref[step]
    ki = ki_ref[step]
    nsub = SUB // 8

    @pl.when(ki == 0)
    def _():
        m_sc[...] = jnp.full(m_sc.shape, NEG_BIG, _f32)
        acc_sc[...] = jnp.zeros(acc_sc.shape, _f32)

    def stage_qk(unit, st, visible):
        kh, hd, c, mode = unit
        hs = slice(hd * HEAD_PAD, (hd + 1) * HEAD_PAD)
        s = _dot(k_ref[0, kh * SUB:(kh + 1) * SUB, hs], q_ref[0, hs, c * SUB:(c + 1) * SUB])
        if mode == "diag":
            s = jnp.where(visible, s, NEG_BIG)
        st["s3"] = s.reshape(nsub, 8, SUB)

    def stage_stats(unit, st, visible):
        kh, hd, c, mode = unit
        cm = jnp.max(st["s3"], axis=0)
        for d in (4, 2, 1):
            cm = jnp.maximum(cm, pltpu.roll(cm, d, 0))
        msl = (slice(hd * 8, (hd + 1) * 8), slice(c * SUB, (c + 1) * SUB))
        m_old = m_sc[msl]
        m_new = jnp.maximum(m_old, cm)
        m_sc[msl] = m_new
        st["alpha"] = jnp.exp2(m_old - m_new)
        st["p"] = jnp.exp2(st.pop("s3") - m_new[None]).reshape(SUB, SUB).astype(_bf16)

    def stage_pv(unit, st, visible):
        kh, hd, c, _ = unit
        v_t = vt_ref[0, hd * MLA_V:(hd + 1) * MLA_V, kh * SUB:(kh + 1) * SUB]
        v_ext = jnp.concatenate([v_t, jnp.ones((ACC_ROWS - MLA_V, SUB), _bf16)], axis=0)
        st["pv"] = _dot(v_ext, st.pop("p")).reshape(ACC_ROWS // 8, 8, SUB)

    def stage_acc(unit, st, visible):
        kh, hd, c, _ = unit
        cs = slice(c * SUB, (c + 1) * SUB)
        acc = acc_sc[hd, :, cs].reshape(ACC_ROWS // 8, 8, SUB)
        acc_sc[hd, :, cs] = (st.pop("alpha")[None] * acc + st.pop("pv")).reshape(ACC_ROWS, SUB)

    stages = ((stage_qk, 0), (stage_stats, LAG_STATS), (stage_pv, LAG_PV), (stage_acc, LAG_ACC))

    def update(rel):
        units = _attn_units(rel)
        visible = None
        if rel is not None:
            row = lax.broadcasted_iota(jnp.int32, (SUB, SUB), 0) // CHUNK
            col = lax.broadcasted_iota(jnp.int32, (SUB, SUB), 1) // CHUNK
            visible = row <= col
        n = len(units)
        state = [{} for _ in units]
        for t in range(n + LAG_ACC):
            for fn, lag in stages:
                if 0 <= t - lag < n:
                    fn(units[t - lag], state[t - lag], visible)

    rel = ki * TK - qi * TQ
    for r in range(0, TQ, TK):
        pl.when(rel == r)(functools.partial(update, r))
    pl.when(rel < 0)(functools.partial(update, None))

    @pl.when(rel == TQ - TK)
    def _():
        for hd in range(MLA_HEADS):
            inv = 1.0 / acc_sc[hd, MLA_V:MLA_V + 8, :]
            o = acc_sc[hd, 0:MLA_V, :].reshape(MLA_V // 8, 8, TQ) * inv[None]
            o_sc[hd * MLA_V:(hd + 1) * MLA_V, :] = o.reshape(MLA_V, TQ)
        o_ref[0] = o_sc[...].T.astype(_bf16)


def _attention(qt, k, vt):
    B, _, S = qt.shape
    nq = S // TQ
    ratio = TQ // TK
    qi_tab, ki_tab = [], []
    for i in range(nq):
        for j in range((i + 1) * ratio):
            qi_tab.append(i)
            ki_tab.append(j)
    n_steps = len(qi_tab)
    qi_tab = jnp.asarray(np.array(qi_tab, np.int32))
    ki_tab = jnp.asarray(np.array(ki_tab, np.int32))
    grid_spec = pltpu.PrefetchScalarGridSpec(
        num_scalar_prefetch=2,
        grid=(B, n_steps),
        in_specs=[
            pl.BlockSpec((1, QK_W, TQ), lambda b, s, qi, ki: (b, 0, qi[s])),
            pl.BlockSpec((1, TK, QK_W), lambda b, s, qi, ki: (b, ki[s], 0)),
            pl.BlockSpec((1, MLA_HEADS * MLA_V, TK), lambda b, s, qi, ki: (b, 0, ki[s])),
        ],
        out_specs=pl.BlockSpec((1, TQ, MLA_HEADS * MLA_V), lambda b, s, qi, ki: (b, qi[s], 0)),
        scratch_shapes=[
            pltpu.VMEM((MLA_HEADS * 8, TQ), _f32),
            pltpu.VMEM((MLA_HEADS, ACC_ROWS, TQ), _f32),
            pltpu.VMEM((MLA_HEADS * MLA_V, TQ), _f32),
        ],
    )
    return pl.pallas_call(
        _attn_kernel,
        out_shape=jax.ShapeDtypeStruct((B, S, MLA_HEADS * MLA_V), _bf16),
        grid_spec=grid_spec,
        compiler_params=pltpu.CompilerParams(
            dimension_semantics=("arbitrary", "arbitrary"), vmem_limit_bytes=VMEM_LIMIT),
        name="mla_attn",
    )(qi_tab, ki_tab, qt, k, vt)


SEG = TS // 8
SEG_PITCH = SEG + 8
LRU_SLABS = LRU_WIDTH // 128


def _rglru_kernel(xb_ref, gg_ref, cw_ref, cb_ref, wr_ref, br_ref, wi_ref, bi_ref, lam_ref,
                  y_ref, seg_sc, hl_sc, cp_sc, tail_sc, carry_sc):
    i = pl.program_id(1)

    @pl.when(i == 0)
    def _():
        tail_sc[...] = jnp.zeros(tail_sc.shape, _f32)
        carry_sc[...] = jnp.zeros(carry_sc.shape, _f32)

    for sl in range(LRU_SLABS):
        for r in range(8):
            seg_sc[sl, r * SEG_PITCH:r * SEG_PITCH + SEG, :] = (
                xb_ref[0, r * SEG:(r + 1) * SEG, sl * 128:(sl + 1) * 128])

    rows = lax.broadcasted_iota(jnp.int32, (8, 128), 0)
    xc_slabs = []
    for sl in range(LRU_SLABS):
        ls = slice(sl * 128, (sl + 1) * 128)
        xg = [seg_sc.at[sl][pl.ds(g, 8, stride=SEG_PITCH), :] for g in range(SEG)]
        before = [jnp.where(rows == 0, tail_sc[j - 1, :, ls], pltpu.roll(xg[SEG - j], 1, 0))
                  for j in range(1, CONV_W)]
        xs = before[::-1] + xg
        xc_g = []
        for g in range(SEG):
            acc = cb_ref[:, ls] + xs[g + CONV_W - 1] * cw_ref[CONV_W - 1:CONV_W, ls]
            for j in range(1, CONV_W):
                acc = acc + xs[g + CONV_W - 1 - j] * cw_ref[CONV_W - 1 - j:CONV_W - j, ls]
            xc_g.append(acc)
        for j in range(1, CONV_W):
            tail_sc[j - 1, :, ls] = jnp.broadcast_to(xg[SEG - j][7:8, :], (8, 128))
        xc_slabs.append(jnp.concatenate(xc_g, axis=0))
    xc = jnp.concatenate(xc_slabs, axis=1)

    xcb = xc.astype(_bf16)
    th_r = jnp.tanh(_dot(xcb, wr_ref[...]) + br_ref[...])
    th_i = jnp.tanh(_dot(xcb, wi_ref[...]) + bi_ref[...])
    lam = lam_ref[...]
    softplus = jnp.maximum(-lam, 0.0) + jnp.log1p(jnp.exp(-jnp.abs(lam)))
    half = (0.5 * LRU_C) * softplus
    nla = th_r * half + half
    a = jnp.exp(-nla)
    z = jnp.tanh(nla) * (a * a + 1.0)
    gain = jnp.where(z > 0.0, z * lax.rsqrt(z), 0.0)
    b = gain * ((0.5 * th_i + 0.5) * xc)

    h = [None] * LRU_SLABS
    cp = [None] * LRU_SLABS
    for g in range(SEG):
        for sl in range(LRU_SLABS):
            ls = slice(sl * 128, (sl + 1) * 128)
            a_g = a[g * 8:(g + 1) * 8, ls]
            b_g = b[g * 8:(g + 1) * 8, ls]
            h[sl] = b_g if g == 0 else a_g * h[sl] + b_g
            cp[sl] = a_g if g == 0 else a_g * cp[sl]
            hl_sc[g, :, ls] = h[sl]
            cp_sc[g, :, ls] = cp[sl]

    for sl in range(LRU_SLABS):
        ls = slice(sl * 128, (sl + 1) * 128)
        a8 = jnp.where(rows == 0, 0.0, pltpu.roll(cp[sl], 1, 0))
        b8 = jnp.where(rows == 0, carry_sc[:, ls], pltpu.roll(h[sl], 1, 0))
        for d in (1, 2, 4):
            ok = rows >= d
            a_sh = pltpu.roll(a8, d, 0)
            b_sh = pltpu.roll(b8, d, 0)
            b8 = jnp.where(ok, a8 * b_sh + b8, b8)
            a8 = jnp.where(ok, a8 * a_sh, a8)
        inflow = b8
        for g in range(SEG):
            hf = hl_sc[g, :, ls] + cp_sc[g, :, ls] * inflow
            seg_sc.at[sl][pl.ds(g, 8, stride=SEG_PITCH), :] = hf
        carry_sc[:, ls] = jnp.broadcast_to(hf[7:8, :], (8, 128))
        for r in range(8):
            rs = slice(r * SEG, (r + 1) * SEG)
            hr = seg_sc[sl, r * SEG_PITCH:r * SEG_PITCH + SEG, :]
            y_ref[0, rs, ls] = (hr * gg_ref[0, rs, ls].astype(_f32)).astype(_bf16)


def _rglru(l, xb, gg, p):
    B, S, W = xb.shape
    tile = pl.BlockSpec((1, TS, W), lambda b, i: (b, i, 0))
    return pl.pallas_call(
        _rglru_kernel,
        out_shape=jax.ShapeDtypeStruct((B, S, W), _bf16),
        grid=(B, S // TS),
        in_specs=[
            tile, tile,
            _layer_spec(l, (CONV_W, W)), _layer_spec(l, (1, W)),
            _layer_spec(l, (W, W)), _layer_spec(l, (1, W)),
            _layer_spec(l, (W, W)), _layer_spec(l, (1, W)),
            _layer_spec(l, (1, W)),
        ],
        out_specs=tile,
        scratch_shapes=[
            pltpu.VMEM((LRU_SLABS, 8 * SEG_PITCH, 128), _f32),
            pltpu.VMEM((SEG, 8, W), _f32),
            pltpu.VMEM((SEG, 8, W), _f32),
            pltpu.VMEM((CONV_W - 1, 8, W), _f32),
            pltpu.VMEM((8, W), _f32),
        ],
        compiler_params=pltpu.CompilerParams(
            dimension_semantics=("arbitrary", "arbitrary"), vmem_limit_bytes=VMEM_LIMIT),
        name="rglru",
    )(xb, gg, p["conv_w"], p["conv_b"], p["wr"], p["br"], p["wi"], p["bi"], p["lam"])


def _merge_kernel(x_ref, mod_ref, ya_ref, yb_ref, yc_ref, zg_ref, bw_ref, mw_ref, lng_ref,
                  lnb_ref, o_ref):
    merged = None
    for n, y_ref in enumerate((ya_ref, yb_ref, yc_ref)):
        gate = _sigmoid(zg_ref[0, :, n * D_MODEL:(n + 1) * D_MODEL].astype(_f32))
        t = gate * _dot(y_ref[0], bw_ref[n])
        merged = t if merged is None else merged + t
    mix = _dot(merged.astype(_bf16), mw_ref[...])
    r = ALPHA * x_ref[0] + mod_ref[0, 2:3, :] * mix
    o_ref[0] = _ln(r) * lng_ref[...] + lnb_ref[...]


def _merge(l, x, mod, ya, yb, yc, zg, p):
    B, S, D = x.shape
    tile = lambda w: pl.BlockSpec((1, TM, w), lambda b, i: (b, i, 0))
    return pl.pallas_call(
        _merge_kernel,
        out_shape=jax.ShapeDtypeStruct((B, S, D), _f32),
        grid=(B, S // TM),
        in_specs=[
            tile(D),
            _mod_spec(l),
            tile(GM_WIDTH), tile(GM_WIDTH), tile(GM_WIDTH), tile(N_BRANCH * D),
            _layer_spec(l, (N_BRANCH, GM_WIDTH, D)),
            _layer_spec(l, (D, D)),
            _layer_spec(l, (1, D)), _layer_spec(l, (1, D)),
        ],
        out_specs=tile(D),
        compiler_params=pltpu.CompilerParams(
            dimension_semantics=("arbitrary", "arbitrary"), vmem_limit_bytes=VMEM_LIMIT),
        name="merge",
    )(x, mod, ya, yb, yc, zg, p["branch_w"], p["mix_w"], p["ln_g0"], p["ln_b0"])


FF_CHUNK = 1024


FF_SUB = 256


def _ffn_kernel(x_ref, mod_ref, w1_ref, b1_ref, w2_ref, b2_ref, lng_ref, lnb_ref, o_ref):
    nsub = TM // FF_SUB
    nchunk = D_FF // FF_CHUNK
    hs = [None] * nsub
    fs = [None] * nsub

    def rows(j):
        return slice(j * FF_SUB, (j + 1) * FF_SUB)

    def pre(j):
        hs[j] = (_ln(x_ref[0, rows(j), :]) * mod_ref[0, 4:5, :] + mod_ref[0, 3:4, :]).astype(_bf16)

    def mm(j, c):
        cs = slice(c * FF_CHUNK, (c + 1) * FF_CHUNK)
        a = jnp.maximum(_dot(hs[j], w1_ref[:, cs]) + b1_ref[:, cs], 0.0)
        t = _dot((a * a).astype(_bf16), w2_ref[cs, :])
        fs[j] = t if c == 0 else fs[j] + t

    def post(j):
        r = ALPHA * x_ref[0, rows(j), :] + mod_ref[0, 5:6, :] * (fs[j] + b2_ref[...])
        o_ref[0, rows(j), :] = _ln(r) * lng_ref[...] + lnb_ref[...]

    pre(0)
    for j in range(nsub):
        for c in range(nchunk):
            mm(j, c)
            if c == 0 and j + 1 < nsub:
                pre(j + 1)
            if c == 1 and j > 0:
                post(j - 1)
    post(nsub - 1)


def _ffn(l, x, mod, p):
    B, S, D = x.shape
    tile = pl.BlockSpec((1, TM, D), lambda b, i: (b, i, 0))
    return pl.pallas_call(
        _ffn_kernel,
        out_shape=jax.ShapeDtypeStruct((B, S, D), _f32),
        grid=(B, S // TM),
        in_specs=[
            tile,
            _mod_spec(l),
            _layer_spec(l, (D, D_FF)), _layer_spec(l, (1, D_FF)),
            _layer_spec(l, (D_FF, D)), _layer_spec(l, (1, D)),
            _layer_spec(l, (1, D)), _layer_spec(l, (1, D)),
        ],
        out_specs=tile,
        compiler_params=pltpu.CompilerParams(
            dimension_semantics=("arbitrary", "arbitrary"), vmem_limit_bytes=VMEM_LIMIT),
        name="ffn",
    )(x, mod, p["w1"], p["b1"], p["w2"], p["b2"], p["ln_g1"], p["ln_b1"])


def _swap_halves(w):
    half = w.shape[-1] // 2
    return jnp.concatenate([w[..., half:], w[..., :half]], axis=-1)


def _head_pad(nope, rope):
    parts = [nope, rope, jnp.zeros(nope.shape[:-1] + (HEAD_PAD - MLA_NOPE - MLA_ROPE,), nope.dtype)]
    out = jnp.concatenate(parts, axis=-1)
    return out.reshape(out.shape[:-2] + (MLA_HEADS * HEAD_PAD,))


def _block_diag(w):
    n, c, d = w.shape[-3:]
    eye = jnp.eye(n, dtype=w.dtype)
    dense = w[..., :, :, None, :] * eye[:, None, :, None]
    return dense.reshape(w.shape[:-3] + (n * c, n * d))


def _prepare_params(in_w, in_b, gm_ln_g, gm_ln_b, gm_ws, gm_bs, mla_qnorm_g, mla_wuq,
                    mla_kvnorm_g, mla_wukv, lru_conv_w, lru_conv_b, lru_wr, lru_br, lru_wi,
                    lru_bi, lru_lambda, branch_w, mix_out_w, ffn_w1, ffn_b1, ffn_w2, ffn_b2,
                    ln_g, ln_b):
    L = in_w.shape[0]
    in_b = in_b[:, None, :]

    def lat_cols(t):
        kr = t[..., OFF_KROPE:OFF_LRU_X]
        z32 = jnp.zeros(kr.shape[:-1] + (MLA_ROPE,), t.dtype)
        return jnp.concatenate([t[..., OFF_QLAT:OFF_KROPE], _swap_halves(kr), z32, kr, z32], axis=-1)

    in_w = in_w.astype(_bf16)
    wuq = mla_wuq.reshape(L, MLA_Q_RANK, MLA_HEADS, MLA_NOPE + MLA_ROPE)
    wuq = jnp.concatenate([wuq, _swap_halves(wuq[..., MLA_NOPE:])], axis=-1)
    wukv = mla_wukv.reshape(L, MLA_KV_RANK, MLA_HEADS, MLA_NOPE + MLA_V)
    wuk = _head_pad(wukv[..., :MLA_NOPE], jnp.zeros((L, MLA_KV_RANK, MLA_HEADS, MLA_ROPE), _f32))
    wuvt = wukv[..., MLA_NOPE:].reshape(L, MLA_KV_RANK, MLA_HEADS * MLA_V).transpose(0, 2, 1)
    row = lambda v: v.reshape(L, 1, -1)
    return dict(
        w_a=in_w[:, :, :OFF_QLAT], b_a=in_b[:, :, :OFF_QLAT],
        w_lat=lat_cols(in_w), b_lat=lat_cols(in_b),
        w_b=in_w[:, :, OFF_LRU_X:], b_b=in_b[:, :, OFF_LRU_X:],
        gm_g=row(gm_ln_g), gm_b=row(gm_ln_b),
        gm_ws=gm_ws, gm_bst=gm_bs.transpose(0, 2, 1),
        q_g=row(mla_qnorm_g),
        wuqt=wuq.reshape(L, MLA_Q_RANK, QK_W).transpose(0, 2, 1).astype(_bf16),
        kv_g=row(mla_kvnorm_g),
        wuk=wuk.astype(_bf16), wuvt=wuvt.astype(_bf16),
        conv_w=lru_conv_w, conv_b=row(lru_conv_b),
        wr=(0.5 * _block_diag(lru_wr)).astype(_bf16), br=0.5 * row(lru_br),
        wi=(0.5 * _block_diag(lru_wi)).astype(_bf16), bi=0.5 * row(lru_bi),
        lam=row(lru_lambda),
        branch_w=branch_w.astype(_bf16), mix_w=mix_out_w.astype(_bf16),
        w1=ffn_w1.astype(_bf16), b1=row(ffn_b1),
        w2=ffn_w2.astype(_bf16), b2=row(ffn_b2),
        ln_g0=row(ln_g[:, 0]), ln_b0=row(ln_b[:, 0]),
        ln_g1=row(ln_g[:, 1]), ln_b1=row(ln_b[:, 1]),
    )


def _rope_tables(seq):
    pos = jnp.arange(seq, dtype=_f32)
    inv = ROPE_BASE ** (-jnp.arange(0, MLA_ROPE, 2, dtype=_f32) / MLA_ROPE)
    ang = pos[:, None] * inv[None, :]
    cos, sin = jnp.cos(ang), jnp.sin(ang)
    ones = jnp.ones((seq, MLA_NOPE), _f32)
    zeros = jnp.zeros((seq, MLA_NOPE), _f32)
    pad = jnp.zeros((seq, HEAD_PAD - MLA_NOPE - MLA_ROPE), _f32)
    cos_q = jnp.concatenate([ones, cos, cos, pad], axis=1)
    cos_k = jnp.concatenate([zeros, cos, cos, pad], axis=1)
    sin_t = jnp.concatenate([zeros, -sin, sin, pad], axis=1)
    return cos_k, sin_t, cos_q.T, sin_t.T


def kernel(x, c, ada_w, ada_b, in_w, in_b, gm_ln_g, gm_ln_b, gm_ws, gm_bs, mla_qnorm_g, mla_wuq, mla_kvnorm_g, mla_wukv, lru_conv_w, lru_conv_b, lru_wr, lru_br, lru_wi, lru_bi, lru_lambda, branch_w, mix_out_w, ffn_w1, ffn_b1, ffn_w2, ffn_b2, ln_g, ln_b):
    B, S, D = x.shape
    assert (B, S, D) == (c.shape[0], S, D_MODEL) and S % TQ == 0 and S % TM == 0
    rope = _rope_tables(S)
    p = _prepare_params(in_w, in_b, gm_ln_g, gm_ln_b, gm_ws, gm_bs, mla_qnorm_g, mla_wuq,
                        mla_kvnorm_g, mla_wukv, lru_conv_w, lru_conv_b, lru_wr, lru_br, lru_wi,
                        lru_bi, lru_lambda, branch_w, mix_out_w, ffn_w1, ffn_b1, ffn_w2, ffn_b2,
                        ln_g, ln_b)
    c8 = jnp.concatenate([c, jnp.zeros((8 - B, D), c.dtype)], axis=0)
    mod_all = _modulation(c8, ada_w, ada_b.reshape(DEPTH, 1, 6 * D))
    m6 = mod_all[:, :B].reshape(DEPTH, B, 6, D)
    m6 = m6 + jnp.array([0.0, 1.0, 1.0, 0.0, 1.0, 1.0], _f32).reshape(1, 1, 6, 1)
    mod = jnp.concatenate([m6, jnp.zeros((DEPTH, B, 2, D), _f32)], axis=2)
    for l in range(DEPTH):
        ya, q, k, vt, xb, gg, zg = _in_proj(l, x, mod, p, rope)
        yb = _attention(q, k, vt)
        yc = _rglru(l, xb, gg, p)
        x = _merge(l, x, mod, ya, yb, yc, zg, p)
        x = _ffn(l, x, mod, p)
    return x
```
